```python
import math
import jax, jax.numpy as jnp
from jax import lax
import numpy as np

D_MODEL = 1024
BATCH = 16
SEQ = 4096
DEPTH = 4
DEC_BATCH = 4
DEC_SEQ = 8192
PAST_LEN = 128

N_MIXERS = 2
N_GLA_LAYERS = (DEPTH + N_MIXERS - 1) // N_MIXERS
N_MLA_LAYERS = DEPTH // N_MIXERS
D_FF = 2816
RES_HALF = 0.5
N_MOD = 9
EPS = 1e-6
GLA_HEADS = 4
GLA_DK = D_MODEL // 2 // GLA_HEADS
GLA_DV = D_MODEL // GLA_HEADS
GLA_GATE_RANK = 16
GLA_TAU = 16.0
GLA_CHUNK = 64
GLA_QK = GLA_HEADS * GLA_DK
GLA_VR = GLA_HEADS * GLA_DV
GLA_IN = 2 * GLA_QK + 2 * GLA_VR + 2 * GLA_GATE_RANK
MLA_HEADS = 8
MLA_NOPE = 128
MLA_ROPE = 64
MLA_V = 128
MLA_Q_RANK = 256
MLA_KV_RANK = 128
MLA_IN = MLA_Q_RANK + MLA_KV_RANK + MLA_ROPE
ROPE_THETA = 10000.0
Q_BLOCK = 128

kernel_name = 'hybrid_gla_mla_macaron_adaln_encoder'


def rms_norm(x, g):
    xf = x.astype(jnp.float32)
    y = xf * lax.rsqrt(jnp.mean(xf * xf, axis=-1, keepdims=True) + EPS)
    return (y * g.astype(jnp.float32)).astype(x.dtype)


def modulate(h, shift, scale):
    return h * (1 + scale[:, None, :]) + shift[:, None, :]


def swiglu_ffn(h, w_gate, w_up, w_down):
    return (jax.nn.silu(h @ w_gate) * (h @ w_up)) @ w_down


def gla_chunked(q, k, v, log_a):
    B, H, S, dk = q.shape
    dv = v.shape[-1]
    n = S // GLA_CHUNK

    def chunks(t):
        return t.reshape(B, H, n, GLA_CHUNK, t.shape[-1])

    q, k, v, log_a = chunks(q), chunks(k), chunks(v), chunks(log_a)
    b = lax.cumsum(log_a, axis=3)
    b_last = b[:, :, :, -1:, :]
    q_in = q * jnp.exp(b)
    k_in = k * jnp.exp(-b)
    k_out = k * jnp.exp(b_last - b)
    lower_tri = jnp.tril(jnp.ones((GLA_CHUNK, GLA_CHUNK), dtype=bool))
    a = jnp.einsum('bhnid,bhnjd->bhnij', q_in, k_in)
    a = jnp.where(lower_tri, a, 0.0)
    o_intra = jnp.einsum('bhnij,bhnjv->bhniv', a, v)

    def step(state, inp):
        q_c, k_c, v_c, decay = inp
        o_c = jnp.einsum('bhid,bhdv->bhiv', q_c, state)
        state = state * decay[..., None] + jnp.einsum('bhjd,bhjv->bhdv', k_c, v_c)
        return state, o_c

    decay = jnp.exp(b_last[:, :, :, 0, :])
    xs = (jnp.moveaxis(q_in, 2, 0), jnp.moveaxis(k_out, 2, 0),
          jnp.moveaxis(v, 2, 0), jnp.moveaxis(decay, 2, 0))
    state0 = jnp.zeros((B, H, dk, dv), jnp.float32)
    _, o_inter = lax.scan(step, state0, xs)
    o = o_intra + jnp.moveaxis(o_inter, 0, 2)
    return o.reshape(B, H, S, dv)


def gla_mixer(h, w_in, w_gate_up, b_gate, g_norm, w_out):
    B, S, _ = h.shape
    proj = h @ w_in
    cuts = [GLA_QK, 2 * GLA_QK, 2 * GLA_QK + GLA_VR, 2 * GLA_QK + 2 * GLA_VR,
            2 * GLA_QK + 2 * GLA_VR + GLA_GATE_RANK]
    q, k, v, r, a_fwd, a_bwd = jnp.split(proj, cuts, axis=-1)

    def heads(t, d):
        return t.reshape(B, S, GLA_HEADS, d).transpose(0, 2, 1, 3).astype(jnp.float32)

    q = heads(q, GLA_DK) * (GLA_DK ** -0.5)
    k = heads(k, GLA_DK)
    v = heads(v, GLA_DV)

    def log_gate(a_low, w_up, b):
        logits = (a_low @ w_up + b).astype(jnp.float32)
        return heads(jax.nn.log_sigmoid(logits) / GLA_TAU, GLA_DK)

    la_fwd = log_gate(a_fwd, w_gate_up[0], b_gate[0])
    la_bwd = log_gate(a_bwd, w_gate_up[1], b_gate[1])
    o_fwd = gla_chunked(q, k, v, la_fwd)
    flip = lambda t: jnp.flip(t, axis=2)
    o_bwd = flip(gla_chunked(flip(q), flip(k), flip(v), flip(la_bwd)))
    o = (o_fwd + o_bwd).transpose(0, 2, 1, 3)
    o = rms_norm(o, g_norm).reshape(B, S, GLA_VR).astype(h.dtype)
    return (jax.nn.silu(r) * o) @ w_out


def rope_tables(S):
    inv = 1.0 / (ROPE_THETA ** (jnp.arange(0, MLA_ROPE, 2, dtype=jnp.float32) / MLA_ROPE))
    ang = jnp.arange(S, dtype=jnp.float32)[:, None] * inv[None, :]
    return jnp.cos(ang)[:, None, :], jnp.sin(ang)[:, None, :]


def apply_rope(x, cos, sin):
    half = MLA_ROPE // 2
    x1, x2 = x[..., :half], x[..., half:]
    out = jnp.concatenate([x1 * cos - x2 * sin, x1 * sin + x2 * cos], axis=-1)
    return out.astype(x.dtype)


def mla_mixer(h, w_in, g_q, g_kv, w_uq, w_ukv, w_out):
    B, S, _ = h.shape
    c_q, c_kv, k_rope = jnp.split(h @ w_in, [MLA_Q_RANK, MLA_Q_RANK + MLA_KV_RANK], axis=-1)
    c_q = rms_norm(c_q, g_q)
    c_kv = rms_norm(c_kv, g_kv)
    q = (c_q @ w_uq).reshape(B, S, MLA_HEADS, MLA_NOPE + MLA_ROPE)
    q_nope, q_rope = q[..., :MLA_NOPE], q[..., MLA_NOPE:]
    kv = (c_kv @ w_ukv).reshape(B, S, MLA_HEADS, MLA_NOPE + MLA_V)
    k_nope, v = kv[..., :MLA_NOPE], kv[..., MLA_NOPE:]
    cos, sin = rope_tables(S)
    q_rope = apply_rope(q_rope, cos, sin)
    k_rope = apply_rope(k_rope[:, :, None, :], cos, sin)[:, :, 0, :]
    scale = (MLA_NOPE + MLA_ROPE) ** -0.5
    nb = S // Q_BLOCK

    def blocks(t):
        return jnp.moveaxis(t.reshape(B, nb, Q_BLOCK, MLA_HEADS, t.shape[-1]), 1, 0)

    def attend(qs):
        qn, qr = qs
        s = (jnp.einsum('bqhd,bkhd->bhqk', qn, k_nope, preferred_element_type=jnp.float32)
             + jnp.einsum('bqhr,bkr->bhqk', qr, k_rope, preferred_element_type=jnp.float32)) * scale
        p = jax.nn.softmax(s, axis=-1).astype(v.dtype)
        return jnp.einsum('bhqk,bkhd->bqhd', p, v)

    o = lax.map(attend, (blocks(q_nope), blocks(q_rope)))
    o = jnp.moveaxis(o, 0, 1).reshape(B, S, MLA_HEADS * MLA_V)
    return o @ w_out


def trunk(x, c, ada_w, ada_b, norm_g, ffn_w_gate, ffn_w_up, ffn_w_down,
          gla_w_in, gla_w_gate_up, gla_b_gate, gla_g_norm, gla_w_out,
          mla_w_in, mla_g_q, mla_g_kv, mla_w_uq, mla_w_ukv, mla_w_out,
          final_ada_w, final_ada_b, final_g):
    c_act = jax.nn.silu(c)
    for i in range(DEPTH):
        mod = c_act @ ada_w[i] + ada_b[i]
        s1, sc1, g1, sm, scm, gm, s2, sc2, g2 = jnp.split(mod, N_MOD, axis=-1)
        h = modulate(rms_norm(x, norm_g[i, 0]), s1, sc1)
        x = x + RES_HALF * g1[:, None, :] * swiglu_ffn(h, ffn_w_gate[i, 0], ffn_w_up[i, 0], ffn_w_down[i, 0])
        h = modulate(rms_norm(x, norm_g[i, 1]), sm, scm)
        j = i // N_MIXERS
        if i % N_MIXERS == 0:
            y = gla_mixer(h, gla_w_in[j], gla_w_gate_up[j], gla_b_gate[j], gla_g_norm[j], gla_w_out[j])
        else:
            y = mla_mixer(h, mla_w_in[j], mla_g_q[j], mla_g_kv[j], mla_w_uq[j], mla_w_ukv[j], mla_w_out[j])
        x = x + gm[:, None, :] * y
        h = modulate(rms_norm(x, norm_g[i, 2]), s2, sc2)
        x = x + RES_HALF * g2[:, None, :] * swiglu_ffn(h, ffn_w_gate[i, 1], ffn_w_up[i, 1], ffn_w_down[i, 1])
    fin_shift, fin_scale = jnp.split(c_act @ final_ada_w + final_ada_b, 2, axis=-1)
    return modulate(rms_norm(x, final_g), fin_shift, fin_scale)


def setup_inputs(seed: int = 0) -> dict:
    key = jax.random.key(seed)
    ks = jax.random.split(key, 24)
    f32 = jnp.float32

    def w(k, shape, fan_in):
        return jax.random.normal(k, shape, f32) * (fan_in ** -0.5)

    def gain(k, shape):
        return 1.0 + 0.02 * jax.random.normal(k, shape, f32)

    def bias(k, shape):
        return 0.02 * jax.random.normal(k, shape, f32)

    D = D_MODEL
    return {
        'x_prompt': jax.random.normal(ks[0], (BATCH, SEQ, D), f32),
        'x_sample': jax.random.normal(ks[1], (DEC_BATCH, DEC_SEQ, D), f32),
        'c_prompt': jax.random.normal(ks[2], (BATCH, D), f32),
        'c_sample': jax.random.normal(ks[3], (DEC_BATCH, D), f32),
        'ada_w': w(ks[4], (DEPTH, D, N_MOD * D), D),
        'ada_b': bias(ks[5], (DEPTH, N_MOD * D)),
        'norm_g': gain(ks[6], (DEPTH, 3, D)),
        'ffn_w_gate': w(ks[7], (DEPTH, 2, D, D_FF), D),
        'ffn_w_up': w(ks[8], (DEPTH, 2, D, D_FF), D),
        'ffn_w_down': w(ks[9], (DEPTH, 2, D_FF, D), D_FF),
        'gla_w_in': w(ks[10], (N_GLA_LAYERS, D, GLA_IN), D),
        'gla_w_gate_up': w(ks[11], (N_GLA_LAYERS, 2, GLA_GATE_RANK, GLA_QK), GLA_GATE_RANK),
        'gla_b_gate': bias(ks[12], (N_GLA_LAYERS, 2, GLA_QK)),
        'gla_g_norm': gain(ks[13], (N_GLA_LAYERS, GLA_DV)),
        'gla_w_out': w(ks[14], (N_GLA_LAYERS, GLA_VR, D), GLA_VR),
        'mla_w_in': w(ks[15], (N_MLA_LAYERS, D, MLA_IN), D),
        'mla_g_q': gain(ks[16], (N_MLA_LAYERS, MLA_Q_RANK)),
        'mla_g_kv': gain(ks[17], (N_MLA_LAYERS, MLA_KV_RANK)),
        'mla_w_uq': w(ks[18], (N_MLA_LAYERS, MLA_Q_RANK, MLA_HEADS * (MLA_NOPE + MLA_ROPE)), MLA_Q_RANK),
        'mla_w_ukv': w(ks[19], (N_MLA_LAYERS, MLA_KV_RANK, MLA_HEADS * (MLA_NOPE + MLA_V)), MLA_KV_RANK),
        'mla_w_out': w(ks[20], (N_MLA_LAYERS, MLA_HEADS * MLA_V, D), MLA_HEADS * MLA_V),
        'final_ada_w': w(ks[21], (D, 2 * D), D),
        'final_ada_b': bias(ks[22], (2 * D,)),
        'final_g': gain(ks[23], (D,)),
    }


def reference(x_prompt, x_sample, c_prompt, c_sample, ada_w, ada_b, norm_g,
              ffn_w_gate, ffn_w_up, ffn_w_down,
              gla_w_in, gla_w_gate_up, gla_b_gate, gla_g_norm, gla_w_out,
              mla_w_in, mla_g_q, mla_g_kv, mla_w_uq, mla_w_ukv, mla_w_out,
              final_ada_w, final_ada_b, final_g):
    y_prompt = trunk(x_prompt, c_prompt, ada_w, ada_b, norm_g, ffn_w_gate, ffn_w_up, ffn_w_down,
                     gla_w_in, gla_w_gate_up, gla_b_gate, gla_g_norm, gla_w_out,
                     mla_w_in, mla_g_q, mla_g_kv, mla_w_uq, mla_w_ukv, mla_w_out,
                     final_ada_w, final_ada_b, final_g)
    y_sample = trunk(x_sample, c_sample, ada_w, ada_b, norm_g, ffn_w_gate, ffn_w_up, ffn_w_down,
                     gla_w_in, gla_w_gate_up, gla_b_gate, gla_g_norm, gla_w_out,
                     mla_w_in, mla_g_q, mla_g_kv, mla_w_uq, mla_w_ukv, mla_w_out,
                     final_ada_w, final_ada_b, final_g)
    return (y_prompt, y_sample)
```

```python
import functools

import jax
import jax.numpy as jnp
from jax import lax
from jax.experimental import pallas as pl
from jax.experimental.pallas import tpu as pltpu

F32 = jnp.float32
BF16 = jnp.bfloat16

D_MODEL = 1024
DEPTH = 4
D_FF = 2816
N_MOD = 9
EPS = 1e-6
RES_HALF = 0.5

GLA_HEADS = 4
GLA_DK = 128
GLA_DV = 256
GLA_GATE_RANK = 16
GLA_TAU = 16.0
GLA_CHUNK = 64
GLA_QK = GLA_HEADS * GLA_DK
GLA_VR = GLA_HEADS * GLA_DV

MLA_HEADS = 8
MLA_NOPE = 128
MLA_ROPE = 64
MLA_V = 128
MLA_Q_RANK = 256
MLA_KV_RANK = 128
ROPE_THETA = 10000.0
MLA_QK_PAD = 256

LANES = 128
VMEM_LIMIT = 56 * 1024 * 1024


def _cparams(n_grid):
    return pltpu.CompilerParams(
        dimension_semantics=("arbitrary",) * n_grid,
        vmem_limit_bytes=VMEM_LIMIT)


def _silu(x):
    return x / (1.0 + jnp.exp(-x))


def _rms(x, g):
    ms = jnp.mean(x * x, axis=-1, keepdims=True)
    return x * lax.rsqrt(ms + EPS) * g


def _norm_mod(x, g, shift, scale):
    return _rms(x, g) * (1.0 + scale) + shift


def _dot(a, b):
    return jnp.dot(a, b, preferred_element_type=F32)


def _dot_nt(a, b):
    return lax.dot_general(a, b, (((1,), (1,)), ((), ())), preferred_element_type=F32)


def _dot_tn(a, b):
    return lax.dot_general(a, b, (((0,), (0,)), ((), ())), preferred_element_type=F32)


def _mod_kernel(c_ref, w_ref, b_ref, o_ref):
    c = _silu(c_ref[...]).astype(BF16)
    o_ref[...] = _dot(c, w_ref[...].astype(BF16)) + b_ref[...]


def _ada_proj(c, w, b, tn=1024):
    n_layers, d, n = w.shape
    bt = c.shape[0]
    return pl.pallas_call(
        _mod_kernel,
        grid=(n_layers, n // tn),
        in_specs=[
            pl.BlockSpec((bt, d), lambda l, j: (0, 0)),
            pl.BlockSpec((None, d, tn), lambda l, j: (l, 0, j)),
            pl.BlockSpec((None, 1, tn), lambda l, j: (l, 0, j)),
        ],
        out_specs=pl.BlockSpec((None, bt, tn), lambda l, j: (l, 0, j)),
        out_shape=jax.ShapeDtypeStruct((n_layers, bt, n), F32),
        compiler_params=_cparams(2),
        name="ada_proj",
    )(c, w, b.reshape(n_layers, 1, n))


def _ffn_kernel(x_ref, sh_ref, sc_ref, gt_ref, g_ref, wg_ref, wu_ref, wd_ref, o_ref):
    x = x_ref[...]
    h = _norm_mod(x, g_ref[...], sh_ref[...], sc_ref[...]).astype(BF16)
    gate = _dot(h, wg_ref[...])
    up = _dot(h, wu_ref[...])
    a = (_silu(gate) * up).astype(BF16)
    y = _dot(a, wd_ref[...])
    o_ref[...] = x + (RES_HALF * gt_ref[...]) * y


def _row_spec(tm, d):
    return pl.BlockSpec((None, tm, d), lambda b, i: (b, i, 0))


def _vec_spec(d):
    return pl.BlockSpec((None, 1, d), lambda b, i: (b, 0, 0))


def _const_spec(shape):
    nd = len(shape)
    return pl.BlockSpec(shape, lambda b, i: (0,) * nd, pipeline_mode=pl.Buffered(1))


def _ffn(x, shift, scale, gate, g, wg, wu, wd, tm):
    bsz, s, d = x.shape
    f = wg.shape[1]
    return pl.pallas_call(
        _ffn_kernel,
        grid=(bsz, s // tm),
        in_specs=[
            _row_spec(tm, d), _vec_spec(d), _vec_spec(d), _vec_spec(d),
            _const_spec((1, d)), _const_spec((d, f)), _const_spec((d, f)),
            _const_spec((f, d)),
        ],
        out_specs=_row_spec(tm, d),
        out_shape=jax.ShapeDtypeStruct(x.shape, F32),
        compiler_params=_cparams(2),
        name="ffn",
    )(x, shift, scale, gate, g, wg, wu, wd)


def _gla_in_kernel(x_ref, sh_ref, sc_ref, g_ref, w_ref, wa_ref,
                   q_ref, k_ref, v_ref, r_ref, a_ref):
    h = _norm_mod(x_ref[...], g_ref[...], sh_ref[...], sc_ref[...]).astype(BF16)
    q_ref[...] = _dot(h, w_ref[:, 0:GLA_QK]) * (GLA_DK ** -0.5)
    k_ref[...] = _dot(h, w_ref[:, GLA_QK:2 * GLA_QK])
    v_ref[...] = _dot(h, w_ref[:, 2 * GLA_QK:2 * GLA_QK + GLA_VR]).astype(BF16)
    r_ref[...] = _dot(h, w_ref[:, 2 * GLA_QK + GLA_VR:])
    a_ref[...] = _dot(h, wa_ref[...]).astype(BF16)


def _gla_in(x, shift, scale, g, w_qkvr, w_a, tm):
    bsz, s, d = x.shape
    n = w_qkvr.shape[1]
    return pl.pallas_call(
        _gla_in_kernel,
        grid=(bsz, s // tm),
        in_specs=[
            _row_spec(tm, d), _vec_spec(d), _vec_spec(d), _const_spec((1, d)),
            _const_spec((d, n)), _const_spec((d, LANES)),
        ],
        out_specs=[
            _row_spec(tm, GLA_QK), _row_spec(tm, GLA_QK), _row_spec(tm, GLA_VR),
            _row_spec(tm, GLA_VR), _row_spec(tm, LANES),
        ],
        out_shape=[
            jax.ShapeDtypeStruct((bsz, s, GLA_QK), F32),
            jax.ShapeDtypeStruct((bsz, s, GLA_QK), F32),
            jax.ShapeDtypeStruct((bsz, s, GLA_VR), BF16),
            jax.ShapeDtypeStruct((bsz, s, GLA_VR), F32),
            jax.ShapeDtypeStruct((bsz, s, LANES), BF16),
        ],
        compiler_params=_cparams(2),
        name="gla_in",
    )(x, shift, scale, g, w_qkvr, w_a)


def _log_sigmoid(x):
    return jnp.minimum(x, 0.0) - jnp.log(1.0 + jnp.exp(-jnp.abs(x)))


def _chunk_cumsum(tri, la):
    hi = la.astype(BF16)
    lo = (la - hi.astype(F32)).astype(BF16)
    return _dot(tri, hi) + _dot(tri, lo)


def _gla_chunk(q, k, v, la, tri, keep, last_row, state_ref, o_ref, rows):
    b = _chunk_cumsum(tri, la)
    eb = jnp.exp(b)
    q_in = (q * eb).astype(BF16)
    k_dec = k * jnp.exp(-b)
    dec = eb[last_row:last_row + 1, :]
    k_in = k_dec.astype(BF16)
    k_out = (k_dec * dec).astype(BF16)
    a = jnp.where(keep, _dot_nt(q_in, k_in), 0.0).astype(BF16)
    state = state_ref[...]
    o_ref[rows, :] = _dot(a, v) + _dot_nt(q_in, state.astype(BF16))
    state_ref[...] = state * dec + _dot_tn(v, k_out)


def _gla_core_kernel(qf_ref, kf_ref, vf_ref, af_ref, qb_ref, kb_ref, vb_ref, ab_ref,
                     wf_ref, wb_ref, bg_ref, of_ref, ob_ref, sf_ref, sb_ref, *, n_chunks):
    @pl.when(pl.program_id(2) == 0)
    def _():
        sf_ref[...] = jnp.zeros_like(sf_ref)
        sb_ref[...] = jnp.zeros_like(sb_ref)

    c = GLA_CHUNK
    la_f = _log_sigmoid(_dot(af_ref[...], wf_ref[...]) + bg_ref[0:1, :]) / GLA_TAU
    la_b = _log_sigmoid(_dot(ab_ref[...], wb_ref[...]) + bg_ref[1:2, :]) / GLA_TAU
    row = lax.broadcasted_iota(jnp.int32, (c, c), 0)
    col = lax.broadcasted_iota(jnp.int32, (c, c), 1)
    lower = col <= row
    upper = col >= row
    tri_f = lower.astype(BF16)
    tri_b = upper.astype(BF16)
    for i in range(n_chunks):
        rf = slice(i * c, (i + 1) * c)
        _gla_chunk(qf_ref[rf, :], kf_ref[rf, :], vf_ref[rf, :], la_f[rf, :],
                   tri_f, lower, c - 1, sf_ref, of_ref, rf)
        j = n_chunks - 1 - i
        rb = slice(j * c, (j + 1) * c)
        _gla_chunk(qb_ref[rb, :], kb_ref[rb, :], vb_ref[rb, :], la_b[rb, :],
                   tri_b, upper, 0, sb_ref, ob_ref, rb)


def _gla_core(q, k, v, a, w_up_f, w_up_b, b_gate, ts):
    bsz, s, _ = q.shape
    nt = s // ts
    fwd = lambda b, h, t: (b, t, h)
    bwd = lambda b, h, t: (b, nt - 1 - t, h)
    fwd0 = lambda b, h, t: (b, t, 0)
    bwd0 = lambda b, h, t: (b, nt - 1 - t, 0)
    head = lambda b, h, t: (0, h)
    qk = lambda m: pl.BlockSpec((None, ts, GLA_DK), m)
    vv = lambda m: pl.BlockSpec((None, ts, GLA_DV), m)
    return pl.pallas_call(
        functools.partial(_gla_core_kernel, n_chunks=ts // GLA_CHUNK),
        grid=(bsz, GLA_HEADS, nt),
        in_specs=[
            qk(fwd), qk(fwd), vv(fwd), pl.BlockSpec((None, ts, LANES), fwd0),
            qk(bwd), qk(bwd), vv(bwd), pl.BlockSpec((None, ts, LANES), bwd0),
            pl.BlockSpec((LANES, GLA_DK), head), pl.BlockSpec((LANES, GLA_DK), head),
            pl.BlockSpec((2, GLA_DK), head),
        ],
        out_specs=[vv(fwd), vv(bwd)],
        out_shape=[jax.ShapeDtypeStruct((bsz, s, GLA_VR), F32)] * 2,
        scratch_shapes=[pltpu.VMEM((GLA_DV, GLA_DK), F32)] * 2,
        compiler_params=_cparams(3),
        name="gla_core",
    )(q, k, v, a, q, k, v, a, w_up_f, w_up_b, b_gate)


def _gla_out_kernel(x_ref, of_ref, ob_ref, r_ref, gm_ref, gn_ref, w_ref, o_ref):
    o = of_ref[...] + ob_ref[...]
    gn = gn_ref[...]
    normed = [_rms(o[:, h * GLA_DV:(h + 1) * GLA_DV], gn) for h in range(GLA_HEADS)]
    z = (_silu(r_ref[...]) * jnp.concatenate(normed, axis=1)).astype(BF16)
    o_ref[...] = x_ref[...] + gm_ref[...] * _dot(z, w_ref[...])


def _gla_out(x, o_f, o_b, r, gm, g_norm, w_out, tm):
    bsz, s, d = x.shape
    return pl.pallas_call(
        _gla_out_kernel,
        grid=(bsz, s // tm),
        in_specs=[
            _row_spec(tm, d), _row_spec(tm, GLA_VR), _row_spec(tm, GLA_VR),
            _row_spec(tm, GLA_VR), _vec_spec(d), _const_spec((1, GLA_DV)),
            _const_spec((GLA_VR, d)),
        ],
        out_specs=_row_spec(tm, d),
        out_shape=jax.ShapeDtypeStruct(x.shape, F32),
        compiler_params=_cparams(2),
        name="gla_out",
    )(x, o_f, o_b, r, gm, g_norm, w_out)


def _rope(x, cos, sin_lo, sin_hi):
    half = MLA_ROPE // 2
    return (x * cos + pltpu.roll(x, LANES - half, 1) * sin_lo
            + pltpu.roll(x, half, 1) * sin_hi)


def _mla_in_kernel(x_ref, sh_ref, sc_ref, g_ref, wq_ref, wkv_ref, wkr_ref, gq_ref, gkv_ref,
                   wuq_ref, wuk_ref, wuv_ref, cos_ref, slo_ref, shi_ref,
                   q_ref, k_ref, v_ref):
    h = _norm_mod(x_ref[...], g_ref[...], sh_ref[...], sc_ref[...]).astype(BF16)
    c_q = _rms(_dot(h, wq_ref[...]), gq_ref[...]).astype(BF16)
    c_kv = _rms(_dot(h, wkv_ref[...]), gkv_ref[...]).astype(BF16)
    cos, slo, shi = cos_ref[...], slo_ref[...], shi_ref[...]
    k_rope = _rope(_dot(h, wkr_ref[...]), cos, slo, shi).astype(BF16)
    scale = (MLA_NOPE + MLA_ROPE) ** -0.5
    v_ref[...] = _dot(c_kv, wuv_ref[...]).astype(BF16)
    for hd in range(MLA_HEADS):
        lo = hd * MLA_QK_PAD
        q_h = _dot(c_q, wuq_ref[:, lo:lo + MLA_QK_PAD])
        q_ref[:, lo:lo + LANES] = (q_h[:, :LANES] * scale).astype(BF16)
        q_ref[:, lo + LANES:lo + MLA_QK_PAD] = (
            _rope(q_h[:, LANES:], cos, slo, shi) * scale).astype(BF16)
        k_ref[:, lo:lo + LANES] = _dot(
            c_kv, wuk_ref[:, hd * MLA_NOPE:(hd + 1) * MLA_NOPE]).astype(BF16)
        k_ref[:, lo + LANES:lo + MLA_QK_PAD] = k_rope


def _mla_in(x, shift, scale, g, p, rope_tabs, tm):
    bsz, s, d = x.shape
    hq = MLA_HEADS * MLA_QK_PAD
    hv = MLA_HEADS * MLA_V
    tab = pl.BlockSpec((tm, LANES), lambda b, i: (i, 0))
    return pl.pallas_call(
        _mla_in_kernel,
        grid=(bsz, s // tm),
        in_specs=[
            _row_spec(tm, d), _vec_spec(d), _vec_spec(d), _const_spec((1, d)),
            _const_spec((d, MLA_Q_RANK)), _const_spec((d, MLA_KV_RANK)),
            _const_spec((d, LANES)), _const_spec((1, MLA_Q_RANK)),
            _const_spec((1, MLA_KV_RANK)), _const_spec((MLA_Q_RANK, hq)),
            _const_spec((MLA_KV_RANK, hv)), _const_spec((MLA_KV_RANK, hv)),
            tab, tab, tab,
        ],
        out_specs=[_row_spec(tm, hq), _row_spec(tm, hq), _row_spec(tm, hv)],
        out_shape=[
            jax.ShapeDtypeStruct((bsz, s, hq), BF16),
            jax.ShapeDtypeStruct((bsz, s, hq), BF16),
            jax.ShapeDtypeStruct((bsz, s, hv), BF16),
        ],
        compiler_params=_cparams(2),
        name="mla_in",
    )(x, shift, scale, g, p["w_cq"], p["w_ckv"], p["w_kr"], p["g_q"], p["g_kv"],
      p["w_uq"], p["w_uk"], p["w_uv"], *rope_tabs)


def _mla_attn_kernel(q_ref, k_ref, v_ref, o_ref, *, tk, n_kv):
    q = q_ref[...]
    tq = q.shape[0]

    def body(j, carry):
        m, l, acc = carry
        start = pl.multiple_of(j * tk, tk)
        s = _dot_nt(q, k_ref[pl.ds(start, tk), :])
        m_new = jnp.maximum(m, jnp.max(s, axis=-1, keepdims=True))
        alpha = jnp.exp(m - m_new)
        p = jnp.exp(s - m_new)
        l = l * alpha + jnp.sum(p, axis=-1, keepdims=True)
        acc = acc * alpha + _dot(p.astype(BF16), v_ref[pl.ds(start, tk), :])
        return m_new, l, acc

    init = (jnp.full((tq, 1), -jnp.inf, F32), jnp.zeros((tq, 1), F32),
            jnp.zeros((tq, MLA_V), F32))
    _, l, acc = lax.fori_loop(0, n_kv, body, init)
    o_ref[...] = (acc / l).astype(BF16)


def _mla_attn(q, k, v, tq, tk):
    bsz, s, _ = q.shape
    return pl.pallas_call(
        functools.partial(_mla_attn_kernel, tk=tk, n_kv=s // tk),
        grid=(bsz, MLA_HEADS, s // tq),
        in_specs=[
            pl.BlockSpec((None, tq, MLA_QK_PAD), lambda b, h, i: (b, i, h)),
            pl.BlockSpec((None, s, MLA_QK_PAD), lambda b, h, i: (b, 0, h)),
            pl.BlockSpec((None, s, MLA_V), lambda b, h, i: (b, 0, h)),
        ],
        out_specs=pl.BlockSpec((None, tq, MLA_V), lambda b, h, i: (b, i, h)),
        out_shape=jax.ShapeDtypeStruct((bsz, s, MLA_HEADS * MLA_V), BF16),
        compiler_params=_cparams(3),
        name="mla_attn",
    )(q, k, v)


def _mla_out_kernel(x_ref, o_ref_in, gm_ref, w_ref, o_ref):
    o_ref[...] = x_ref[...] + gm_ref[...] * _dot(o_ref_in[...], w_ref[...])


def _mla_out(x, o, gm, w_out, tm):
    bsz, s, d = x.shape
    hv = MLA_HEADS * MLA_V
    return pl.pallas_call(
        _mla_out_kernel,
        grid=(bsz, s // tm),
        in_specs=[_row_spec(tm, d), _row_spec(tm, hv), _vec_spec(d), _const_spec((hv, d))],
        out_specs=_row_spec(tm, d),
        out_shape=jax.ShapeDtypeStruct(x.shape, F32),
        compiler_params=_cparams(2),
        name="mla_out",
    )(x, o, gm, w_out)


def _final_kernel(x_ref, sh_ref, sc_ref, g_ref, o_ref):
    o_ref[...] = _norm_mod(x_ref[...], g_ref[...], sh_ref[...], sc_ref[...])


def _final(x, shift, scale, g, tm):
    bsz, s, d = x.shape
    return pl.pallas_call(
        _final_kernel,
        grid=(bsz, s // tm),
        in_specs=[_row_spec(tm, d), _vec_spec(d), _vec_spec(d), _const_spec((1, d))],
        out_specs=_row_spec(tm, d),
        out_shape=jax.ShapeDtypeStruct(x.shape, F32),
        compiler_params=_cparams(2),
        name="final_mod",
    )(x, shift, scale, g)


def _rope_tables(s):
    half = MLA_ROPE // 2
    inv = 1.0 / (ROPE_THETA ** (jnp.arange(0, MLA_ROPE, 2, dtype=F32) / MLA_ROPE))
    ang = jnp.arange(s, dtype=F32)[:, None] * inv[None, :]
    cos, sin = jnp.cos(ang), jnp.sin(ang)
    z = jnp.zeros((s, half), F32)
    z2 = jnp.zeros((s, LANES - MLA_ROPE), F32)
    return (jnp.concatenate([cos, cos, z2], axis=1),
            jnp.concatenate([-sin, z, z2], axis=1),
            jnp.concatenate([z, sin, z2], axis=1))


def _prep_gla(w_in, w_gate_up, b_gate, g_norm, w_out):
    n_main = 2 * GLA_QK + 2 * GLA_VR
    w_a = jnp.pad(w_in[:, n_main:], ((0, 0), (0, LANES - 2 * GLA_GATE_RANK)))
    r = GLA_GATE_RANK
    w_up_f = jnp.pad(w_gate_up[0], ((0, LANES - r), (0, 0)))
    w_up_b = jnp.pad(w_gate_up[1], ((r, LANES - 2 * r), (0, 0)))
    return dict(w_qkvr=w_in[:, :n_main].astype(BF16), w_a=w_a.astype(BF16),
                w_up_f=w_up_f.astype(BF16), w_up_b=w_up_b.astype(BF16),
                b_gate=b_gate, g_norm=g_norm.reshape(1, GLA_DV), w_out=w_out.astype(BF16))


def _prep_mla(w_in, g_q, g_kv, w_uq, w_ukv, w_out):
    d = w_in.shape[0]
    w_kr = jnp.pad(w_in[:, MLA_Q_RANK + MLA_KV_RANK:], ((0, 0), (0, LANES - MLA_ROPE)))
    uq = w_uq.reshape(MLA_Q_RANK, MLA_HEADS, MLA_NOPE + MLA_ROPE)
    uq = jnp.pad(uq, ((0, 0), (0, 0), (0, MLA_QK_PAD - MLA_NOPE - MLA_ROPE)))
    ukv = w_ukv.reshape(MLA_KV_RANK, MLA_HEADS, MLA_NOPE + MLA_V)
    return dict(
        w_cq=w_in[:, :MLA_Q_RANK].astype(BF16),
        w_ckv=w_in[:, MLA_Q_RANK:MLA_Q_RANK + MLA_KV_RANK].astype(BF16),
        w_kr=w_kr.astype(BF16),
        g_q=g_q.reshape(1, MLA_Q_RANK), g_kv=g_kv.reshape(1, MLA_KV_RANK),
        w_uq=uq.reshape(MLA_Q_RANK, MLA_HEADS * MLA_QK_PAD).astype(BF16),
        w_uk=ukv[:, :, :MLA_NOPE].reshape(MLA_KV_RANK, MLA_HEADS * MLA_NOPE).astype(BF16),
        w_uv=ukv[:, :, MLA_NOPE:].reshape(MLA_KV_RANK, MLA_HEADS * MLA_V).astype(BF16),
        w_out=w_out.astype(BF16))


def _tile(s, want):
    return min(s, want)


def _trunk(x, mods, fin, norm_g, final_g, ffn_w, gla_p, mla_p):
    bsz, s, d = x.shape
    tm = _tile(s, 512)
    rope_tabs = _rope_tables(s)
    vec = lambda m: m.reshape(bsz, 1, d)
    for i in range(DEPTH):
        s1, sc1, g1, sm, scm, gm, s2, sc2, g2 = [
            vec(mods[i, :, j * d:(j + 1) * d]) for j in range(N_MOD)]
        wg, wu, wd = ffn_w
        x = _ffn(x, s1, sc1, g1, norm_g[i, 0][None], wg[i, 0], wu[i, 0], wd[i, 0], tm)
        j = i // 2
        if i % 2 == 0:
            p = gla_p[j]
            q, k, v, r, a = _gla_in(x, sm, scm, norm_g[i, 1][None], p["w_qkvr"], p["w_a"], tm)
            o_f, o_b = _gla_core(q, k, v, a, p["w_up_f"], p["w_up_b"], p["b_gate"],
                                 _tile(s, 512))
            x = _gla_out(x, o_f, o_b, r, gm, p["g_norm"], p["w_out"], tm)
        else:
            p = mla_p[j]
            q, k, v = _mla_in(x, sm, scm, norm_g[i, 1][None], p, rope_tabs, tm)
            o = _mla_attn(q, k, v, _tile(s, 256), _tile(s, 512))
            x = _mla_out(x, o, gm, p["w_out"], tm)
        x = _ffn(x, s2, sc2, g2, norm_g[i, 2][None], wg[i, 1], wu[i, 1], wd[i, 1], tm)
    return _final(x, vec(fin[:, :d]), vec(fin[:, d:]), final_g, tm)


def kernel(x_prompt, x_sample, c_prompt, c_sample, ada_w, ada_b, norm_g, ffn_w_gate, ffn_w_up, ffn_w_down, gla_w_in, gla_w_gate_up, gla_b_gate, gla_g_norm, gla_w_out, mla_w_in, mla_g_q, mla_g_kv, mla_w_uq, mla_w_ukv, mla_w_out, final_ada_w, final_ada_b, final_g):
    nb = x_prompt.shape[0]
    c_all = jnp.concatenate([c_prompt, c_sample], axis=0)
    mods = _ada_proj(c_all, ada_w, ada_b)
    fin = _ada_proj(c_all, final_ada_w[None], final_ada_b[None])[0]
    ffn_w = (ffn_w_gate.astype(BF16), ffn_w_up.astype(BF16), ffn_w_down.astype(BF16))
    gla_p = [_prep_gla(gla_w_in[j], gla_w_gate_up[j], gla_b_gate[j], gla_g_norm[j],
                       gla_w_out[j]) for j in range(gla_w_in.shape[0])]
    mla_p = [_prep_mla(mla_w_in[j], mla_g_q[j], mla_g_kv[j], mla_w_uq[j], mla_w_ukv[j],
                       mla_w_out[j]) for j in range(mla_w_in.shape[0])]
    fg = final_g.reshape(1, -1)
    y_prompt = _trunk(x_prompt, mods[:, :nb], fin[:nb], norm_g, fg, ffn_w, gla_p, mla_p)
    y_sample = _trunk(x_sample, mods[:, nb:], fin[nb:], norm_g, fg, ffn_w, gla_p, mla_p)
    return (y_prompt, y_sample)
```

```python
import functools

import jax
import jax.numpy as jnp
from jax import lax
from jax.experimental import pallas as pl
from jax.experimental.pallas import tpu as pltpu

F32 = jnp.float32
BF16 = jnp.bfloat16

D_MODEL = 1024
DEPTH = 4
D_FF = 2816
N_MOD = 9
EPS = 1e-6
RES_HALF = 0.5

GLA_HEADS = 4
GLA_DK = 128
GLA_DV = 256
GLA_GATE_RANK = 16
GLA_TAU = 16.0
GLA_CHUNK = 64
GLA_QK = GLA_HEADS * GLA_DK
GLA_VR = GLA_HEADS * GLA_DV

MLA_HEADS = 8
MLA_NOPE = 128
MLA_ROPE = 64
MLA_V = 128
MLA_Q_RANK = 256
MLA_KV_RANK = 128
ROPE_THETA = 10000.0
MLA_QK_PAD = 256
LOG2_E = 1.4426950408889634

LANES = 128
VMEM_LIMIT = 56 * 1024 * 1024


def _cparams(n_grid):
    return pltpu.CompilerParams(
        dimension_semantics=("arbitrary",) * n_grid,
        vmem_limit_bytes=VMEM_LIMIT)


def _silu(x):
    return x / (1.0 + jnp.exp(-x))


def _rms(x, g):
    ms = jnp.mean(x * x, axis=-1, keepdims=True)
    return x * lax.rsqrt(ms + EPS) * g


def _norm_mod(x, g, shift, scale):
    return _rms(x, g) * (1.0 + scale) + shift


def _dot(a, b):
    return jnp.dot(a, b, preferred_element_type=F32)


def _dot_nt(a, b):
    return lax.dot_general(a, b, (((1,), (1,)), ((), ())), preferred_element_type=F32)


def _dot_tn(a, b):
    return lax.dot_general(a, b, (((0,), (0,)), ((), ())), preferred_element_type=F32)


def _mod_kernel(c_ref, w_ref, b_ref, o_ref):
    c = _silu(c_ref[...]).astype(BF16)
    o_ref[...] = _dot(c, w_ref[...].astype(BF16)) + b_ref[...]


def _ada_proj(c, w, b, tn=1024):
    n_layers, d, n = w.shape
    bt = c.shape[0]
    return pl.pallas_call(
        _mod_kernel,
        grid=(n_layers, n // tn),
        in_specs=[
            pl.BlockSpec((bt, d), lambda l, j: (0, 0)),
            pl.BlockSpec((None, d, tn), lambda l, j: (l, 0, j)),
            pl.BlockSpec((None, 1, tn), lambda l, j: (l, 0, j)),
        ],
        out_specs=pl.BlockSpec((None, bt, tn), lambda l, j: (l, 0, j)),
        out_shape=jax.ShapeDtypeStruct((n_layers, bt, n), F32),
        compiler_params=_cparams(2),
        name="ada_proj",
    )(c, w, b.reshape(n_layers, 1, n))


def _ffn_kernel(x_ref, sh_ref, sc_ref, gt_ref, g_ref, wg_ref, wu_ref, wd_ref, o_ref):
    x = x_ref[...]
    h = _norm_mod(x, g_ref[...], sh_ref[...], sc_ref[...]).astype(BF16)
    gate = _dot(h, wg_ref[...])
    up = _dot(h, wu_ref[...])
    a = (_silu(gate) * up).astype(BF16)
    y = _dot(a, wd_ref[...])
    o_ref[...] = x + (RES_HALF * gt_ref[...]) * y


def _row_spec(tm, d):
    return pl.BlockSpec((None, tm, d), lambda b, i: (b, i, 0))


def _vec_spec(d):
    return pl.BlockSpec((None, 1, d), lambda b, i: (b, 0, 0))


def _const_spec(shape):
    nd = len(shape)
    return pl.BlockSpec(shape, lambda b, i: (0,) * nd, pipeline_mode=pl.Buffered(1))


def _ffn(x, shift, scale, gate, g, wg, wu, wd, tm):
    bsz, s, d = x.shape
    f = wg.shape[1]
    return pl.pallas_call(
        _ffn_kernel,
        grid=(bsz, s // tm),
        in_specs=[
            _row_spec(tm, d), _vec_spec(d), _vec_spec(d), _vec_spec(d),
            _const_spec((1, d)), _const_spec((d, f)), _const_spec((d, f)),
            _const_spec((f, d)),
        ],
        out_specs=_row_spec(tm, d),
        out_shape=jax.ShapeDtypeStruct(x.shape, F32),
        compiler_params=_cparams(2),
        name="ffn",
    )(x, shift, scale, gate, g, wg, wu, wd)


def _gla_in_kernel(x_ref, sh_ref, sc_ref, g_ref, w_ref, wa_ref,
                   q_ref, k_ref, v_ref, r_ref, a_ref):
    h = _norm_mod(x_ref[...], g_ref[...], sh_ref[...], sc_ref[...]).astype(BF16)
    q_ref[...] = _dot(h, w_ref[:, 0:GLA_QK]) * (GLA_DK ** -0.5)
    k_ref[...] = _dot(h, w_ref[:, GLA_QK:2 * GLA_QK])
    v_ref[...] = _dot(h, w_ref[:, 2 * GLA_QK:2 * GLA_QK + GLA_VR]).astype(BF16)
    r_ref[...] = _dot(h, w_ref[:, 2 * GLA_QK + GLA_VR:])
    a_ref[...] = _dot(h, wa_ref[...]).astype(BF16)


def _gla_in(x, shift, scale, g, w_qkvr, w_a, tm):
    bsz, s, d = x.shape
    n = w_qkvr.shape[1]
    return pl.pallas_call(
        _gla_in_kernel,
        grid=(bsz, s // tm),
        in_specs=[
            _row_spec(tm, d), _vec_spec(d), _vec_spec(d), _const_spec((1, d)),
            _const_spec((d, n)), _const_spec((d, LANES)),
        ],
        out_specs=[
            _row_spec(tm, GLA_QK), _row_spec(tm, GLA_QK), _row_spec(tm, GLA_VR),
            _row_spec(tm, GLA_VR), _row_spec(tm, LANES),
        ],
        out_shape=[
            jax.ShapeDtypeStruct((bsz, s, GLA_QK), F32),
            jax.ShapeDtypeStruct((bsz, s, GLA_QK), F32),
            jax.ShapeDtypeStruct((bsz, s, GLA_VR), BF16),
            jax.ShapeDtypeStruct((bsz, s, GLA_VR), F32),
            jax.ShapeDtypeStruct((bsz, s, LANES), BF16),
        ],
        compiler_params=_cparams(2),
        name="gla_in",
    )(x, shift, scale, g, w_qkvr, w_a)


def _log_sigmoid(x):
    return jnp.minimum(x, 0.0) - jnp.log(1.0 + jnp.exp(-jnp.abs(x)))


def _chunk_cumsum(tri, la):
    hi = la.astype(BF16)
    lo = (la - hi.astype(F32)).astype(BF16)
    return _dot(tri, hi) + _dot(tri, lo)


def _gla_chunk(q, k, v, la, tri, keep, last_row, state_ref, o_ref, rows):
    b = _chunk_cumsum(tri, la)
    eb = jnp.exp(b)
    q_in = (q * eb).astype(BF16)
    k_dec = k * jnp.exp(-b)
    dec = eb[last_row:last_row + 1, :]
    k_in = k_dec.astype(BF16)
    k_out = (k_dec * dec).astype(BF16)
    a = jnp.where(keep, _dot_nt(q_in, k_in), 0.0).astype(BF16)
    state = state_ref[...]
    o_ref[rows, :] = _dot(a, v) + _dot_nt(q_in, state.astype(BF16))
    state_ref[...] = state * dec + _dot_tn(v, k_out)


def _gla_core_kernel(qf_ref, kf_ref, vf_ref, af_ref, qb_ref, kb_ref, vb_ref, ab_ref,
                     wf_ref, wb_ref, bg_ref, of_ref, ob_ref, sf_ref, sb_ref, *, n_chunks):
    @pl.when(pl.program_id(2) == 0)
    def _():
        sf_ref[...] = jnp.zeros_like(sf_ref)
        sb_ref[...] = jnp.zeros_like(sb_ref)

    c = GLA_CHUNK
    la_f = _log_sigmoid(_dot(af_ref[...], wf_ref[...]) + bg_ref[0:1, :]) / GLA_TAU
    la_b = _log_sigmoid(_dot(ab_ref[...], wb_ref[...]) + bg_ref[1:2, :]) / GLA_TAU
    row = lax.broadcasted_iota(jnp.int32, (c, c), 0)
    col = lax.broadcasted_iota(jnp.int32, (c, c), 1)
    lower = col <= row
    upper = col >= row
    tri_f = lower.astype(BF16)
    tri_b = upper.astype(BF16)
    for i in range(n_chunks):
        rf = slice(i * c, (i + 1) * c)
        _gla_chunk(qf_ref[rf, :], kf_ref[rf, :], vf_ref[rf, :], la_f[rf, :],
                   tri_f, lower, c - 1, sf_ref, of_ref, rf)
        j = n_chunks - 1 - i
        rb = slice(j * c, (j + 1) * c)
        _gla_chunk(qb_ref[rb, :], kb_ref[rb, :], vb_ref[rb, :], la_b[rb, :],
                   tri_b, upper, 0, sb_ref, ob_ref, rb)


def _gla_core(q, k, v, a, w_up_f, w_up_b, b_gate, ts):
    bsz, s, _ = q.shape
    nt = s // ts
    fwd = lambda b, h, t: (b, t, h)
    bwd = lambda b, h, t: (b, nt - 1 - t, h)
    fwd0 = lambda b, h, t: (b, t, 0)
    bwd0 = lambda b, h, t: (b, nt - 1 - t, 0)
    head = lambda b, h, t: (0, h)
    qk = lambda m: pl.BlockSpec((None, ts, GLA_DK), m)
    vv = lambda m: pl.BlockSpec((None, ts, GLA_DV), m)
    return pl.pallas_call(
        functools.partial(_gla_core_kernel, n_chunks=ts // GLA_CHUNK),
        grid=(bsz, GLA_HEADS, nt),
        in_specs=[
            qk(fwd), qk(fwd), vv(fwd), pl.BlockSpec((None, ts, LANES), fwd0),
            qk(bwd), qk(bwd), vv(bwd), pl.BlockSpec((None, ts, LANES), bwd0),
            pl.BlockSpec((LANES, GLA_DK), head), pl.BlockSpec((LANES, GLA_DK), head),
            pl.BlockSpec((2, GLA_DK), head),
        ],
        out_specs=[vv(fwd), vv(bwd)],
        out_shape=[jax.ShapeDtypeStruct((bsz, s, GLA_VR), F32)] * 2,
        scratch_shapes=[pltpu.VMEM((GLA_DV, GLA_DK), F32)] * 2,
        compiler_params=_cparams(3),
        name="gla_core",
    )(q, k, v, a, q, k, v, a, w_up_f, w_up_b, b_gate)


def _gla_out_kernel(x_ref, of_ref, ob_ref, r_ref, gm_ref, gn_ref, w_ref, o_ref):
    o = of_ref[...] + ob_ref[...]
    gn = gn_ref[...]
    normed = [_rms(o[:, h * GLA_DV:(h + 1) * GLA_DV], gn) for h in range(GLA_HEADS)]
    z = (_silu(r_ref[...]) * jnp.concatenate(normed, axis=1)).astype(BF16)
    o_ref[...] = x_ref[...] + gm_ref[...] * _dot(z, w_ref[...])


def _gla_out(x, o_f, o_b, r, gm, g_norm, w_out, tm):
    bsz, s, d = x.shape
    return pl.pallas_call(
        _gla_out_kernel,
        grid=(bsz, s // tm),
        in_specs=[
            _row_spec(tm, d), _row_spec(tm, GLA_VR), _row_spec(tm, GLA_VR),
            _row_spec(tm, GLA_VR), _vec_spec(d), _const_spec((1, GLA_DV)),
            _const_spec((GLA_VR, d)),
        ],
        out_specs=_row_spec(tm, d),
        out_shape=jax.ShapeDtypeStruct(x.shape, F32),
        compiler_params=_cparams(2),
        name="gla_out",
    )(x, o_f, o_b, r, gm, g_norm, w_out)


def _rope(x, cos, sin_lo, sin_hi):
    half = MLA_ROPE // 2
    return (x * cos + pltpu.roll(x, LANES - half, 1) * sin_lo
            + pltpu.roll(x, half, 1) * sin_hi)


def _mla_in_kernel(x_ref, sh_ref, sc_ref, g_ref, wq_ref, wkv_ref, wkr_ref, gq_ref, gkv_ref,
                   wuq_ref, wuk_ref, wuvt_ref, cos_ref, slo_ref, shi_ref,
                   q_ref, k_ref, vt_ref):
    h = _norm_mod(x_ref[...], g_ref[...], sh_ref[...], sc_ref[...]).astype(BF16)
    c_q = _rms(_dot(h, wq_ref[...]), gq_ref[...]).astype(BF16)
    c_kv = _rms(_dot(h, wkv_ref[...]), gkv_ref[...]).astype(BF16)
    cos, slo, shi = cos_ref[...], slo_ref[...], shi_ref[...]
    k_rope = _rope(_dot(h, wkr_ref[...]), cos, slo, shi).astype(BF16)
    scale = (MLA_NOPE + MLA_ROPE) ** -0.5 * LOG2_E
    vt_ref[...] = _dot_nt(wuvt_ref[...], c_kv).astype(BF16)
    for hd in range(MLA_HEADS):
        lo = hd * MLA_QK_PAD
        q_h = _dot(c_q, wuq_ref[:, lo:lo + MLA_QK_PAD])
        q_ref[:, lo:lo + LANES] = (q_h[:, :LANES] * scale).astype(BF16)
        q_ref[:, lo + LANES:lo + MLA_QK_PAD] = (
            _rope(q_h[:, LANES:], cos, slo, shi) * scale).astype(BF16)
        k_ref[:, lo:lo + LANES] = _dot(
            c_kv, wuk_ref[:, hd * MLA_NOPE:(hd + 1) * MLA_NOPE]).astype(BF16)
        k_ref[:, lo + LANES:lo + MLA_QK_PAD] = k_rope


def _mla_in(x, shift, scale, g, p, rope_tabs, tm):
    bsz, s, d = x.shape
    hq = MLA_HEADS * MLA_QK_PAD
    hv = MLA_HEADS * MLA_V
    tab = pl.BlockSpec((tm, LANES), lambda b, i: (i, 0))
    return pl.pallas_call(
        _mla_in_kernel,
        grid=(bsz, s // tm),
        in_specs=[
            _row_spec(tm, d), _vec_spec(d), _vec_spec(d), _const_spec((1, d)),
            _const_spec((d, MLA_Q_RANK)), _const_spec((d, MLA_KV_RANK)),
            _const_spec((d, LANES)), _const_spec((1, MLA_Q_RANK)),
            _const_spec((1, MLA_KV_RANK)), _const_spec((MLA_Q_RANK, hq)),
            _const_spec((MLA_KV_RANK, hv)), _const_spec((hv, MLA_KV_RANK)),
            tab, tab, tab,
        ],
        out_specs=[_row_spec(tm, hq), _row_spec(tm, hq),
                   pl.BlockSpec((None, None, hv, tm), lambda b, i: (b, i, 0, 0))],
        out_shape=[
            jax.ShapeDtypeStruct((bsz, s, hq), BF16),
            jax.ShapeDtypeStruct((bsz, s, hq), BF16),
            jax.ShapeDtypeStruct((bsz, s // tm, hv, tm), BF16),
        ],
        compiler_params=_cparams(2),
        name="mla_in",
    )(x, shift, scale, g, p["w_cq"], p["w_ckv"], p["w_kr"], p["g_q"], p["g_kv"],
      p["w_uq"], p["w_uk"], p["w_uvt"], *rope_tabs)


def _mla_attn_kernel(q_ref, k_ref, vt_ref, o_ref, sa_ref, sb_ref, *, tk, n_kv):
    q = q_ref[...]
    tq = q.shape[0]

    def stage1(j, dst_ref, m):
        start = pl.multiple_of(j * tk, tk)
        dst_ref[...] = _dot_nt(k_ref[pl.ds(start, tk), :], q)
        return jnp.maximum(m, jnp.max(dst_ref[...], axis=0, keepdims=True))

    def stage2(j, src_ref, m_prev, m_cur, l, acc):
        alpha = jnp.exp2(m_prev - m_cur)
        p = jnp.exp2(src_ref[...] - m_cur)
        l = l * alpha + jnp.sum(p, axis=0, keepdims=True)
        acc = acc * alpha + _dot(vt_ref[j], p.astype(BF16))
        return l, acc

    def pair(i, carry, last):
        m_a, m_b, m_c, l, acc = carry
        j = 2 * i
        l, acc = stage2(j, sa_ref, m_a, m_b, l, acc)
        l, acc = stage2(j + 1, sb_ref, m_b, m_c, l, acc)
        if last:
            return l, acc
        m_d = stage1(j + 2, sa_ref, m_c)
        m_e = stage1(j + 3, sb_ref, m_d)
        return m_c, m_d, m_e, l, acc

    neg = jnp.full((1, tq), -jnp.inf, F32)
    m_0 = stage1(0, sa_ref, neg)
    m_1 = stage1(1, sb_ref, m_0)
    init = (neg, m_0, m_1, jnp.zeros((1, tq), F32), jnp.zeros((MLA_V, tq), F32))
    carry = lax.fori_loop(0, n_kv // 2 - 1, functools.partial(pair, last=False), init)
    l, acc = pair(n_kv // 2 - 1, carry, True)
    o_ref[...] = (acc / l).T.astype(BF16)


def _mla_attn(q, k, vt, tq):
    bsz, s, _ = q.shape
    n_kv, tk = vt.shape[1], vt.shape[3]
    assert n_kv % 2 == 0
    return pl.pallas_call(
        functools.partial(_mla_attn_kernel, tk=tk, n_kv=n_kv),
        grid=(bsz, MLA_HEADS, s // tq),
        in_specs=[
            pl.BlockSpec((None, tq, MLA_QK_PAD), lambda b, h, i: (b, i, h)),
            pl.BlockSpec((None, s, MLA_QK_PAD), lambda b, h, i: (b, 0, h)),
            pl.BlockSpec((None, n_kv, MLA_V, tk), lambda b, h, i: (b, 0, h, 0)),
        ],
        out_specs=pl.BlockSpec((None, tq, MLA_V), lambda b, h, i: (b, i, h)),
        out_shape=jax.ShapeDtypeStruct((bsz, s, MLA_HEADS * MLA_V), BF16),
        scratch_shapes=[pltpu.VMEM((tk, tq), F32)] * 2,
        compiler_params=_cparams(3),
        name="mla_attn",
    )(q, k, vt)


def _mla_out_kernel(x_ref, o_ref_in, gm_ref, w_ref, o_ref):
    o_ref[...] = x_ref[...] + gm_ref[...] * _dot(o_ref_in[...], w_ref[...])


def _mla_out(x, o, gm, w_out, tm):
    bsz, s, d = x.shape
    hv = MLA_HEADS * MLA_V
    return pl.pallas_call(
        _mla_out_kernel,
        grid=(bsz, s // tm),
        in_specs=[_row_spec(tm, d), _row_spec(tm, hv), _vec_spec(d), _const_spec((hv, d))],
        out_specs=_row_spec(tm, d),
        out_shape=jax.ShapeDtypeStruct(x.shape, F32),
        compiler_params=_cparams(2),
        name="mla_out",
    )(x, o, gm, w_out)


def _final_kernel(x_ref, sh_ref, sc_ref, g_ref, o_ref):
    o_ref[...] = _norm_mod(x_ref[...], g_ref[...], sh_ref[...], sc_ref[...])


def _final(x, shift, scale, g, tm):
    bsz, s, d = x.shape
    return pl.pallas_call(
        _final_kernel,
        grid=(bsz, s // tm),
        in_specs=[_row_spec(tm, d), _vec_spec(d), _vec_spec(d), _const_spec((1, d))],
        out_specs=_row_spec(tm, d),
        out_shape=jax.ShapeDtypeStruct(x.shape, F32),
        compiler_params=_cparams(2),
        name="final_mod",
    )(x, shift, scale, g)


def _rope_tables(s):
    half = MLA_ROPE // 2
    inv = 1.0 / (ROPE_THETA ** (jnp.arange(0, MLA_ROPE, 2, dtype=F32) / MLA_ROPE))
    ang = jnp.arange(s, dtype=F32)[:, None] * inv[None, :]
    cos, sin = jnp.cos(ang), jnp.sin(ang)
    z = jnp.zeros((s, half), F32)
    z2 = jnp.zeros((s, LANES - MLA_ROPE), F32)
    return (jnp.concatenate([cos, cos, z2], axis=1),
            jnp.concatenate([-sin, z, z2], axis=1),
            jnp.concatenate([z, sin, z2], axis=1))


def _prep_gla(w_in, w_gate_up, b_gate, g_norm, w_out):
    n_main = 2 * GLA_QK + 2 * GLA_VR
    w_a = jnp.pad(w_in[:, n_main:], ((0, 0), (0, LANES - 2 * GLA_GATE_RANK)))
    r = GLA_GATE_RANK
    w_up_f = jnp.pad(w_gate_up[0], ((0, LANES - r), (0, 0)))
    w_up_b = jnp.pad(w_gate_up[1], ((r, LANES - 2 * r), (0, 0)))
    return dict(w_qkvr=w_in[:, :n_main].astype(BF16), w_a=w_a.astype(BF16),
                w_up_f=w_up_f.astype(BF16), w_up_b=w_up_b.astype(BF16),
                b_gate=b_gate, g_norm=g_norm.reshape(1, GLA_DV), w_out=w_out.astype(BF16))


def _prep_mla(w_in, g_q, g_kv, w_uq, w_ukv, w_out):
    d = w_in.shape[0]
    w_kr = jnp.pad(w_in[:, MLA_Q_RANK + MLA_KV_RANK:], ((0, 0), (0, LANES - MLA_ROPE)))
    uq = w_uq.reshape(MLA_Q_RANK, MLA_HEADS, MLA_NOPE + MLA_ROPE)
    uq = jnp.pad(uq, ((0, 0), (0, 0), (0, MLA_QK_PAD - MLA_NOPE - MLA_ROPE)))
    ukv = w_ukv.reshape(MLA_KV_RANK, MLA_HEADS, MLA_NOPE + MLA_V)
    return dict(
        w_cq=w_in[:, :MLA_Q_RANK].astype(BF16),
        w_ckv=w_in[:, MLA_Q_RANK:MLA_Q_RANK + MLA_KV_RANK].astype(BF16),
        w_kr=w_kr.astype(BF16),
        g_q=g_q.reshape(1, MLA_Q_RANK), g_kv=g_kv.reshape(1, MLA_KV_RANK),
        w_uq=uq.reshape(MLA_Q_RANK, MLA_HEADS * MLA_QK_PAD).astype(BF16),
        w_uk=ukv[:, :, :MLA_NOPE].reshape(MLA_KV_RANK, MLA_HEADS * MLA_NOPE).astype(BF16),
        w_uvt=ukv[:, :, MLA_NOPE:].reshape(MLA_KV_RANK, MLA_HEADS * MLA_V).T.astype(BF16),
        w_out=w_out.astype(BF16))


def _tile(s, want):
    return min(s, want)


def _trunk(x, mods, fin, norm_g, final_g, ffn_w, gla_p, mla_p):
    bsz, s, d = x.shape
    tm = _tile(s, 512)
    rope_tabs = _rope_tables(s)
    vec = lambda m: m.reshape(bsz, 1, d)
    for i in range(DEPTH):
        s1, sc1, g1, sm, scm, gm, s2, sc2, g2 = [
            vec(mods[i, :, j * d:(j + 1) * d]) for j in range(N_MOD)]
        wg, wu, wd = ffn_w
        x = _ffn(x, s1, sc1, g1, norm_g[i, 0][None], wg[i, 0], wu[i, 0], wd[i, 0], tm)
        j = i // 2
        if i % 2 == 0:
            p = gla_p[j]
            q, k, v, r, a = _gla_in(x, sm, scm, norm_g[i, 1][None], p["w_qkvr"], p["w_a"], tm)
            o_f, o_b = _gla_core(q, k, v, a, p["w_up_f"], p["w_up_b"], p["b_gate"],
                                 _tile(s, 512))
            x = _gla_out(x, o_f, o_b, r, gm, p["g_norm"], p["w_out"], tm)
        else:
            p = mla_p[j]
            q, k, vt = _mla_in(x, sm, scm, norm_g[i, 1][None], p, rope_tabs, min(tm, s // 2))
            o = _mla_attn(q, k, vt, _tile(s, 512))
            x = _mla_out(x, o, gm, p["w_out"], tm)
        x = _ffn(x, s2, sc2, g2, norm_g[i, 2][None], wg[i, 1], wu[i, 1], wd[i, 1], tm)
    return _final(x, vec(fin[:, :d]), vec(fin[:, d:]), final_g, tm)


def kernel(x_prompt, x_sample, c_prompt, c_sample, ada_w, ada_b, norm_g, ffn_w_gate, ffn_w_up, ffn_w_down, gla_w_in, gla_w_gate_up, gla_b_gate, gla_g_norm, gla_w_out, mla_w_in, mla_g_q, mla_g_kv, mla_w_uq, mla_w_ukv, mla_w_out, final_ada_w, final_ada_b, final_g):
    nb = x_prompt.shape[0]
    c_all = jnp.concatenate([c_prompt, c_sample], axis=0)
    mods = _ada_proj(c_all, ada_w, ada_b)
    fin = _ada_proj(c_all, final_ada_w[None], final_ada_b[None])[0]
    ffn_w = (ffn_w_gate.astype(BF16), ffn_w_up.astype(BF16), ffn_w_down.astype(BF16))
    gla_p = [_prep_gla(gla_w_in[j], gla_w_gate_up[j], gla_b_gate[j], gla_g_norm[j],
                       gla_w_out[j]) for j in range(gla_w_in.shape[0])]
    mla_p = [_prep_mla(mla_w_in[j], mla_g_q[j], mla_g_kv[j], mla_w_uq[j], mla_w_ukv[j],
                       mla_w_out[j]) for j in range(mla_w_in.shape[0])]
    fg = final_g.reshape(1, -1)
    y_prompt = _trunk(x_prompt, mods[:, :nb], fin[:nb], norm_g, fg, ffn_w, gla_p, mla_p)
    y_sample = _trunk(x_sample, mods[:, nb:], fin[nb:], norm_g, fg, ffn_w, gla_p, mla_p)
    return (y_prompt, y_sample)
```

```python
import functools

import jax
import jax.numpy as jnp
from jax import lax
from jax.experimental import pallas as pl
from jax.experimental.pallas import tpu as pltpu

F32 = jnp.float32
BF16 = jnp.bfloat16

D_MODEL = 1024
DEPTH = 4
D_FF = 2816
N_MOD = 9
EPS = 1e-6
RES_HALF = 0.5

GLA_HEADS = 4
GLA_DK = 128
GLA_DV = 256
GLA_GATE_RANK = 16
GLA_TAU = 16.0
GLA_CHUNK = 64
GLA_QK = GLA_HEADS * GLA_DK
GLA_VR = GLA_HEADS * GLA_DV

MLA_HEADS = 8
MLA_NOPE = 128
MLA_ROPE = 64
MLA_V = 128
MLA_Q_RANK = 256
MLA_KV_RANK = 128
ROPE_THETA = 10000.0
MLA_QK_PAD = 256
LOG2_E = 1.4426950408889634
MLA_TK = 512

LANES = 128
VMEM_LIMIT = 56 * 1024 * 1024


def _cparams(n_grid):
    return pltpu.CompilerParams(
        dimension_semantics=("arbitrary",) * n_grid,
        vmem_limit_bytes=VMEM_LIMIT)


def _silu(x):
    return x / (1.0 + jnp.exp(-x))


def _rms(x, g):
    ms = jnp.mean(x * x, axis=-1, keepdims=True)
    return x * lax.rsqrt(ms + EPS) * g


def _norm_mod(x, g, shift, scale):
    return _rms(x, g) * (1.0 + scale) + shift


def _dot(a, b):
    return jnp.dot(a, b, preferred_element_type=F32)


def _dot_nt(a, b):
    return lax.dot_general(a, b, (((1,), (1,)), ((), ())), preferred_element_type=F32)


def _dot_tn(a, b):
    return lax.dot_general(a, b, (((0,), (0,)), ((), ())), preferred_element_type=F32)


def _mod_kernel(c_ref, w_ref, b_ref, o_ref):
    c = _silu(c_ref[...]).astype(BF16)
    o_ref[...] = _dot(c, w_ref[...].astype(BF16)) + b_ref[...]


def _ada_proj(c, w, b, tn=1024):
    n_layers, d, n = w.shape
    bt = c.shape[0]
    return pl.pallas_call(
        _mod_kernel,
        grid=(n_layers, n // tn),
        in_specs=[
            pl.BlockSpec((bt, d), lambda l, j: (0, 0)),
            pl.BlockSpec((None, d, tn), lambda l, j: (l, 0, j)),
            pl.BlockSpec((None, 1, tn), lambda l, j: (l, 0, j)),
        ],
        out_specs=pl.BlockSpec((None, bt, tn), lambda l, j: (l, 0, j)),
        out_shape=jax.ShapeDtypeStruct((n_layers, bt, n), F32),
        compiler_params=_cparams(2),
        name="ada_proj",
    )(c, w, b.reshape(n_layers, 1, n))


def _ffn_kernel(x_ref, sh_ref, sc_ref, gt_ref, g_ref, wg_ref, wu_ref, wd_ref, o_ref):
    x = x_ref[...]
    h = _norm_mod(x, g_ref[...], sh_ref[...], sc_ref[...]).astype(BF16)
    gate = _dot(h, wg_ref[...])
    up = _dot(h, wu_ref[...])
    a = (_silu(gate) * up).astype(BF16)
    y = _dot(a, wd_ref[...])
    o_ref[...] = x + (RES_HALF * gt_ref[...]) * y


def _row_spec(tm, d):
    return pl.BlockSpec((None, tm, d), lambda b, i: (b, i, 0))


def _vec_spec(d):
    return pl.BlockSpec((None, 1, d), lambda b, i: (b, 0, 0))


def _const_spec(shape):
    nd = len(shape)
    return pl.BlockSpec(shape, lambda b, i: (0,) * nd, pipeline_mode=pl.Buffered(1))


def _ffn(x, shift, scale, gate, g, wg, wu, wd, tm):
    bsz, s, d = x.shape
    f = wg.shape[1]
    return pl.pallas_call(
        _ffn_kernel,
        grid=(bsz, s // tm),
        in_specs=[
            _row_spec(tm, d), _vec_spec(d), _vec_spec(d), _vec_spec(d),
            _const_spec((1, d)), _const_spec((d, f)), _const_spec((d, f)),
            _const_spec((f, d)),
        ],
        out_specs=_row_spec(tm, d),
        out_shape=jax.ShapeDtypeStruct(x.shape, F32),
        compiler_params=_cparams(2),
        name="ffn",
    )(x, shift, scale, gate, g, wg, wu, wd)


def _gla_in_kernel(x_ref, sh_ref, sc_ref, g_ref, w_ref, wa_ref,
                   q_ref, k_ref, v_ref, r_ref, a_ref):
    h = _norm_mod(x_ref[...], g_ref[...], sh_ref[...], sc_ref[...]).astype(BF16)
    q_ref[...] = _dot(h, w_ref[:, 0:GLA_QK]) * (GLA_DK ** -0.5)
    k_ref[...] = _dot(h, w_ref[:, GLA_QK:2 * GLA_QK])
    v_ref[...] = _dot(h, w_ref[:, 2 * GLA_QK:2 * GLA_QK + GLA_VR]).astype(BF16)
    r_ref[...] = _dot(h, w_ref[:, 2 * GLA_QK + GLA_VR:])
    a_ref[...] = _dot(h, wa_ref[...]).astype(BF16)


def _gla_in(x, shift, scale, g, w_qkvr, w_a, tm):
    bsz, s, d = x.shape
    n = w_qkvr.shape[1]
    return pl.pallas_call(
        _gla_in_kernel,
        grid=(bsz, s // tm),
        in_specs=[
            _row_spec(tm, d), _vec_spec(d), _vec_spec(d), _const_spec((1, d)),
            _const_spec((d, n)), _const_spec((d, LANES)),
        ],
        out_specs=[
            _row_spec(tm, GLA_QK), _row_spec(tm, GLA_QK), _row_spec(tm, GLA_VR),
            _row_spec(tm, GLA_VR), _row_spec(tm, LANES),
        ],
        out_shape=[
            jax.ShapeDtypeStruct((bsz, s, GLA_QK), F32),
            jax.ShapeDtypeStruct((bsz, s, GLA_QK), F32),
            jax.ShapeDtypeStruct((bsz, s, GLA_VR), BF16),
            jax.ShapeDtypeStruct((bsz, s, GLA_VR), F32),
            jax.ShapeDtypeStruct((bsz, s, LANES), BF16),
        ],
        compiler_params=_cparams(2),
        name="gla_in",
    )(x, shift, scale, g, w_qkvr, w_a)


def _log_sigmoid(x):
    return jnp.minimum(x, 0.0) - jnp.log(1.0 + jnp.exp(-jnp.abs(x)))


def _chunk_cumsum(tri, la):
    hi = la.astype(BF16)
    lo = (la - hi.astype(F32)).astype(BF16)
    return _dot(tri, hi) + _dot(tri, lo)


def _gla_direction(q_ref, k_ref, v_ref, la, tri, keep, last_row, state_ref, o_ref, order):
    c = GLA_CHUNK
    rows = [slice(i * c, (i + 1) * c) for i in order]
    b = [_chunk_cumsum(tri, la[r, :]) for r in rows]
    eb = [jnp.exp(x) for x in b]
    q_in = [(q_ref[r, :] * e).astype(BF16) for r, e in zip(rows, eb)]
    k_dec = [k_ref[r, :] * jnp.exp(-x) for r, x in zip(rows, b)]
    dec = [e[last_row:last_row + 1, :] for e in eb]
    k_out = [(kd * d).astype(BF16) for kd, d in zip(k_dec, dec)]
    a = [jnp.where(keep, _dot_nt(qi, kd.astype(BF16)), 0.0).astype(BF16)
         for qi, kd in zip(q_in, k_dec)]
    o_intra = [_dot(ai, v_ref[r, :]) for ai, r in zip(a, rows)]
    kv = [_dot_tn(v_ref[r, :], ko) for r, ko in zip(rows, k_out)]
    state = state_ref[...]
    for i, r in enumerate(rows):
        o_ref[r, :] = o_intra[i] + _dot_nt(q_in[i], state.astype(BF16))
        state = state * dec[i] + kv[i]
    state_ref[...] = state


def _gla_core_kernel(qf_ref, kf_ref, vf_ref, af_ref, qb_ref, kb_ref, vb_ref, ab_ref,
                     wf_ref, wb_ref, bg_ref, of_ref, ob_ref, sf_ref, sb_ref, *, n_chunks):
    @pl.when(pl.program_id(2) == 0)
    def _():
        sf_ref[...] = jnp.zeros_like(sf_ref)
        sb_ref[...] = jnp.zeros_like(sb_ref)

    c = GLA_CHUNK
    la_f = _log_sigmoid(_dot(af_ref[...], wf_ref[...]) + bg_ref[0:1, :]) / GLA_TAU
    la_b = _log_sigmoid(_dot(ab_ref[...], wb_ref[...]) + bg_ref[1:2, :]) / GLA_TAU
    row = lax.broadcasted_iota(jnp.int32, (c, c), 0)
    col = lax.broadcasted_iota(jnp.int32, (c, c), 1)
    lower = col <= row
    upper = col >= row
    tri_f = lower.astype(BF16)
    tri_b = upper.astype(BF16)
    _gla_direction(qf_ref, kf_ref, vf_ref, la_f, tri_f, lower, c - 1, sf_ref, of_ref,
                   range(n_chunks))
    _gla_direction(qb_ref, kb_ref, vb_ref, la_b, tri_b, upper, 0, sb_ref, ob_ref,
                   range(n_chunks - 1, -1, -1))


def _gla_core(q, k, v, a, w_up_f, w_up_b, b_gate, ts):
    bsz, s, _ = q.shape
    nt = s // ts
    fwd = lambda b, h, t: (b, t, h)
    bwd = lambda b, h, t: (b, nt - 1 - t, h)
    fwd0 = lambda b, h, t: (b, t, 0)
    bwd0 = lambda b, h, t: (b, nt - 1 - t, 0)
    head = lambda b, h, t: (0, h)
    qk = lambda m: pl.BlockSpec((None, ts, GLA_DK), m)
    vv = lambda m: pl.BlockSpec((None, ts, GLA_DV), m)
    return pl.pallas_call(
        functools.partial(_gla_core_kernel, n_chunks=ts // GLA_CHUNK),
        grid=(bsz, GLA_HEADS, nt),
        in_specs=[
            qk(fwd), qk(fwd), vv(fwd), pl.BlockSpec((None, ts, LANES), fwd0),
            qk(bwd), qk(bwd), vv(bwd), pl.BlockSpec((None, ts, LANES), bwd0),
            pl.BlockSpec((LANES, GLA_DK), head), pl.BlockSpec((LANES, GLA_DK), head),
            pl.BlockSpec((2, GLA_DK), head),
        ],
        out_specs=[vv(fwd), vv(bwd)],
        out_shape=[jax.ShapeDtypeStruct((bsz, s, GLA_VR), F32)] * 2,
        scratch_shapes=[pltpu.VMEM((GLA_DV, GLA_DK), F32)] * 2,
        compiler_params=_cparams(3),
        name="gla_core",
    )(q, k, v, a, q, k, v, a, w_up_f, w_up_b, b_gate)


def _gla_out_kernel(x_ref, of_ref, ob_ref, r_ref, gm_ref, gn_ref, w_ref, o_ref):
    o = of_ref[...] + ob_ref[...]
    gn = gn_ref[...]
    normed = [_rms(o[:, h * GLA_DV:(h + 1) * GLA_DV], gn) for h in range(GLA_HEADS)]
    z = (_silu(r_ref[...]) * jnp.concatenate(normed, axis=1)).astype(BF16)
    o_ref[...] = x_ref[...] + gm_ref[...] * _dot(z, w_ref[...])


def _gla_out(x, o_f, o_b, r, gm, g_norm, w_out, tm):
    bsz, s, d = x.shape
    return pl.pallas_call(
        _gla_out_kernel,
        grid=(bsz, s // tm),
        in_specs=[
            _row_spec(tm, d), _row_spec(tm, GLA_VR), _row_spec(tm, GLA_VR),
            _row_spec(tm, GLA_VR), _vec_spec(d), _const_spec((1, GLA_DV)),
            _const_spec((GLA_VR, d)),
        ],
        out_specs=_row_spec(tm, d),
        out_shape=jax.ShapeDtypeStruct(x.shape, F32),
        compiler_params=_cparams(2),
        name="gla_out",
    )(x, o_f, o_b, r, gm, g_norm, w_out)


def _rope(x, cos, sin_lo, sin_hi):
    half = MLA_ROPE // 2
    return (x * cos + pltpu.roll(x, LANES - half, 1) * sin_lo
            + pltpu.roll(x, half, 1) * sin_hi)


def _mla_in_kernel(x_ref, sh_ref, sc_ref, g_ref, wq_ref, wkv_ref, wkr_ref, gq_ref, gkv_ref,
                   wuq_ref, wuk_ref, wuvt_ref, cos_ref, slo_ref, shi_ref,
                   q_ref, k_ref, vt_ref):
    h = _norm_mod(x_ref[...], g_ref[...], sh_ref[...], sc_ref[...]).astype(BF16)
    c_q = _rms(_dot(h, wq_ref[...]), gq_ref[...]).astype(BF16)
    c_kv = _rms(_dot(h, wkv_ref[...]), gkv_ref[...]).astype(BF16)
    cos, slo, shi = cos_ref[...], slo_ref[...], shi_ref[...]
    k_rope = _rope(_dot(h, wkr_ref[...]), cos, slo, shi).astype(BF16)
    scale = (MLA_NOPE + MLA_ROPE) ** -0.5 * LOG2_E
    vt = _dot_nt(wuvt_ref[...], c_kv).astype(BF16)
    for c in range(vt_ref.shape[0]):
        vt_ref[c] = vt[:, c * MLA_TK:(c + 1) * MLA_TK]
    for hd in range(MLA_HEADS):
        lo = hd * MLA_QK_PAD
        q_h = _dot(c_q, wuq_ref[:, lo:lo + MLA_QK_PAD])
        q_ref[:, lo:lo + LANES] = (q_h[:, :LANES] * scale).astype(BF16)
        q_ref[:, lo + LANES:lo + MLA_QK_PAD] = (
            _rope(q_h[:, LANES:], cos, slo, shi) * scale).astype(BF16)
        k_ref[:, lo:lo + LANES] = _dot(
            c_kv, wuk_ref[:, hd * MLA_NOPE:(hd + 1) * MLA_NOPE]).astype(BF16)
        k_ref[:, lo + LANES:lo + MLA_QK_PAD] = k_rope


def _mla_in(x, shift, scale, g, p, rope_tabs, tm):
    bsz, s, d = x.shape
    hq = MLA_HEADS * MLA_QK_PAD
    hv = MLA_HEADS * MLA_V
    tab = pl.BlockSpec((tm, LANES), lambda b, i: (i, 0))
    return pl.pallas_call(
        _mla_in_kernel,
        grid=(bsz, s // tm),
        in_specs=[
            _row_spec(tm, d), _vec_spec(d), _vec_spec(d), _const_spec((1, d)),
            _const_spec((d, MLA_Q_RANK)), _const_spec((d, MLA_KV_RANK)),
            _const_spec((d, LANES)), _const_spec((1, MLA_Q_RANK)),
            _const_spec((1, MLA_KV_RANK)), _const_spec((MLA_Q_RANK, hq)),
            _const_spec((MLA_KV_RANK, hv)), _const_spec((hv, MLA_KV_RANK)),
            tab, tab, tab,
        ],
        out_specs=[_row_spec(tm, hq), _row_spec(tm, hq),
                   pl.BlockSpec((None, tm // MLA_TK, hv, MLA_TK), lambda b, i: (b, i, 0, 0))],
        out_shape=[
            jax.ShapeDtypeStruct((bsz, s, hq), BF16),
            jax.ShapeDtypeStruct((bsz, s, hq), BF16),
            jax.ShapeDtypeStruct((bsz, s // MLA_TK, hv, MLA_TK), BF16),
        ],
        compiler_params=_cparams(2),
        name="mla_in",
    )(x, shift, scale, g, p["w_cq"], p["w_ckv"], p["w_kr"], p["g_q"], p["g_kv"],
      p["w_uq"], p["w_uk"], p["w_uvt"], *rope_tabs)


def _mla_attn_kernel(q_ref, k_ref, vt_ref, o_ref, sa_ref, sb_ref, *, tk, n_kv):
    q = q_ref[...]
    tq = q.shape[0]

    def stage1(j, dst_ref, m):
        start = pl.multiple_of(j * tk, tk)
        dst_ref[...] = _dot_nt(k_ref[pl.ds(start, tk), :], q)
        return jnp.maximum(m, jnp.max(dst_ref[...], axis=0, keepdims=True))

    def stage2(j, src_ref, m_prev, m_cur, l, acc):
        alpha = jnp.exp2(m_prev - m_cur)
        p = jnp.exp2(src_ref[...] - m_cur)
        l = l * alpha + jnp.sum(p, axis=0, keepdims=True)
        acc = acc * alpha + _dot(vt_ref[j], p.astype(BF16))
        return l, acc

    def pair(i, carry, last):
        m_prev, m_cur, l, acc = carry
        j = 2 * i
        m_1 = stage1(j + 1, sb_ref, m_cur)
        l, acc = stage2(j, sa_ref, m_prev, m_cur, l, acc)
        m_2 = m_1 if last else stage1(j + 2, sa_ref, m_1)
        l, acc = stage2(j + 1, sb_ref, m_cur, m_1, l, acc)
        return m_1, m_2, l, acc

    neg = jnp.full((1, tq), -jnp.inf, F32)
    init = (neg, stage1(0, sa_ref, neg), jnp.zeros((1, tq), F32), jnp.zeros((MLA_V, tq), F32))
    carry = lax.fori_loop(0, n_kv // 2 - 1, functools.partial(pair, last=False), init)
    _, _, l, acc = pair(n_kv // 2 - 1, carry, True)
    o_ref[...] = (acc / l).T.astype(BF16)


def _mla_attn(q, k, vt, tq):
    bsz, s, _ = q.shape
    n_kv, tk = vt.shape[1], vt.shape[3]
    assert n_kv % 2 == 0
    return pl.pallas_call(
        functools.partial(_mla_attn_kernel, tk=tk, n_kv=n_kv),
        grid=(bsz, MLA_HEADS, s // tq),
        in_specs=[
            pl.BlockSpec((None, tq, MLA_QK_PAD), lambda b, h, i: (b, i, h)),
            pl.BlockSpec((None, s, MLA_QK_PAD), lambda b, h, i: (b, 0, h)),
            pl.BlockSpec((None, n_kv, MLA_V, tk), lambda b, h, i: (b, 0, h, 0)),
        ],
        out_specs=pl.BlockSpec((None, tq, MLA_V), lambda b, h, i: (b, i, h)),
        out_shape=jax.ShapeDtypeStruct((bsz, s, MLA_HEADS * MLA_V), BF16),
        scratch_shapes=[pltpu.VMEM((tk, tq), F32)] * 2,
        compiler_params=_cparams(3),
        name="mla_attn",
    )(q, k, vt)


def _mla_out_kernel(x_ref, o_ref_in, gm_ref, w_ref, o_ref):
    o_ref[...] = x_ref[...] + gm_ref[...] * _dot(o_ref_in[...], w_ref[...])


def _mla_out(x, o, gm, w_out, tm):
    bsz, s, d = x.shape
    hv = MLA_HEADS * MLA_V
    return pl.pallas_call(
        _mla_out_kernel,
        grid=(bsz, s // tm),
        in_specs=[_row_spec(tm, d), _row_spec(tm, hv), _vec_spec(d), _const_spec((hv, d))],
        out_specs=_row_spec(tm, d),
        out_shape=jax.ShapeDtypeStruct(x.shape, F32),
        compiler_params=_cparams(2),
        name="mla_out",
    )(x, o, gm, w_out)


def _final_kernel(x_ref, sh_ref, sc_ref, g_ref, o_ref):
    o_ref[...] = _norm_mod(x_ref[...], g_ref[...], sh_ref[...], sc_ref[...])


def _final(x, shift, scale, g, tm):
    bsz, s, d = x.shape
    return pl.pallas_call(
        _final_kernel,
        grid=(bsz, s // tm),
        in_specs=[_row_spec(tm, d), _vec_spec(d), _vec_spec(d), _const_spec((1, d))],
        out_specs=_row_spec(tm, d),
        out_shape=jax.ShapeDtypeStruct(x.shape, F32),
        compiler_params=_cparams(2),
        name="final_mod",
    )(x, shift, scale, g)


def _rope_tables(s):
    half = MLA_ROPE // 2
    inv = 1.0 / (ROPE_THETA ** (jnp.arange(0, MLA_ROPE, 2, dtype=F32) / MLA_ROPE))
    ang = jnp.arange(s, dtype=F32)[:, None] * inv[None, :]
    cos, sin = jnp.cos(ang), jnp.sin(ang)
    z = jnp.zeros((s, half), F32)
    z2 = jnp.zeros((s, LANES - MLA_ROPE), F32)
    return (jnp.concatenate([cos, cos, z2], axis=1),
            jnp.concatenate([-sin, z, z2], axis=1),
            jnp.concatenate([z, sin, z2], axis=1))


def _prep_gla(w_in, w_gate_up, b_gate, g_norm, w_out):
    n_main = 2 * GLA_QK + 2 * GLA_VR
    w_a = jnp.pad(w_in[:, n_main:], ((0, 0), (0, LANES - 2 * GLA_GATE_RANK)))
    r = GLA_GATE_RANK
    w_up_f = jnp.pad(w_gate_up[0], ((0, LANES - r), (0, 0)))
    w_up_b = jnp.pad(w_gate_up[1], ((r, LANES - 2 * r), (0, 0)))
    return dict(w_qkvr=w_in[:, :n_main].astype(BF16), w_a=w_a.astype(BF16),
                w_up_f=w_up_f.astype(BF16), w_up_b=w_up_b.astype(BF16),
                b_gate=b_gate, g_norm=g_norm.reshape(1, GLA_DV), w_out=w_out.astype(BF16))


def _prep_mla(w_in, g_q, g_kv, w_uq, w_ukv, w_out):
    d = w_in.shape[0]
    w_kr = jnp.pad(w_in[:, MLA_Q_RANK + MLA_KV_RANK:], ((0, 0), (0, LANES - MLA_ROPE)))
    uq = w_uq.reshape(MLA_Q_RANK, MLA_HEADS, MLA_NOPE + MLA_ROPE)
    uq = jnp.pad(uq, ((0, 0), (0, 0), (0, MLA_QK_PAD - MLA_NOPE - MLA_ROPE)))
    ukv = w_ukv.reshape(MLA_KV_RANK, MLA_HEADS, MLA_NOPE + MLA_V)
    return dict(
        w_cq=w_in[:, :MLA_Q_RANK].astype(BF16),
        w_ckv=w_in[:, MLA_Q_RANK:MLA_Q_RANK + MLA_KV_RANK].astype(BF16),
        w_kr=w_kr.astype(BF16),
        g_q=g_q.reshape(1, MLA_Q_RANK), g_kv=g_kv.reshape(1, MLA_KV_RANK),
        w_uq=uq.reshape(MLA_Q_RANK, MLA_HEADS * MLA_QK_PAD).astype(BF16),
        w_uk=ukv[:, :, :MLA_NOPE].reshape(MLA_KV_RANK, MLA_HEADS * MLA_NOPE).astype(BF16),
        w_uvt=ukv[:, :, MLA_NOPE:].reshape(MLA_KV_RANK, MLA_HEADS * MLA_V).T.astype(BF16),
        w_out=w_out.astype(BF16))


def _tile(s, want):
    return min(s, want)


def _trunk(x, mods, fin, norm_g, final_g, ffn_w, gla_p, mla_p):
    bsz, s, d = x.shape
    tm = _tile(s, 512)
    rope_tabs = _rope_tables(s)
    vec = lambda m: m.reshape(bsz, 1, d)
    for i in range(DEPTH):
        s1, sc1, g1, sm, scm, gm, s2, sc2, g2 = [
            vec(mods[i, :, j * d:(j + 1) * d]) for j in range(N_MOD)]
        wg, wu, wd = ffn_w
        x = _ffn(x, s1, sc1, g1, norm_g[i, 0][None], wg[i, 0], wu[i, 0], wd[i, 0], tm)
        j = i // 2
        if i % 2 == 0:
            p = gla_p[j]
            q, k, v, r, a = _gla_in(x, sm, scm, norm_g[i, 1][None], p["w_qkvr"], p["w_a"], tm)
            o_f, o_b = _gla_core(q, k, v, a, p["w_up_f"], p["w_up_b"], p["b_gate"],
                                 _tile(s, 512))
            x = _gla_out(x, o_f, o_b, r, gm, p["g_norm"], p["w_out"], tm)
        else:
            p = mla_p[j]
            q, k, vt = _mla_in(x, sm, scm, norm_g[i, 1][None], p, rope_tabs, min(tm, s // 2))
            o = _mla_attn(q, k, vt, _tile(s, 1024))
            x = _mla_out(x, o, gm, p["w_out"], tm)
        x = _ffn(x, s2, sc2, g2, norm_g[i, 2][None], wg[i, 1], wu[i, 1], wd[i, 1], tm)
    return _final(x, vec(fin[:, :d]), vec(fin[:, d:]), final_g, tm)


def kernel(x_prompt, x_sample, c_prompt, c_sample, ada_w, ada_b, norm_g, ffn_w_gate, ffn_w_up, ffn_w_down, gla_w_in, gla_w_gate_up, gla_b_gate, gla_g_norm, gla_w_out, mla_w_in, mla_g_q, mla_g_kv, mla_w_uq, mla_w_ukv, mla_w_out, final_ada_w, final_ada_b, final_g):
    nb = x_prompt.shape[0]
    c_all = jnp.concatenate([c_prompt, c_sample], axis=0)
    mods = _ada_proj(c_all, ada_w, ada_b)
    fin = _ada_proj(c_all, final_ada_w[None], final_ada_b[None])[0]
    ffn_w = (ffn_w_gate.astype(BF16), ffn_w_up.astype(BF16), ffn_w_down.astype(BF16))
    gla_p = [_prep_gla(gla_w_in[j], gla_w_gate_up[j], gla_b_gate[j], gla_g_norm[j],
                       gla_w_out[j]) for j in range(gla_w_in.shape[0])]
    mla_p = [_prep_mla(mla_w_in[j], mla_g_q[j], mla_g_kv[j], mla_w_uq[j], mla_w_ukv[j],
                       mla_w_out[j]) for j in range(mla_w_in.shape[0])]
    fg = final_g.reshape(1, -1)
    y_prompt = _trunk(x_prompt, mods[:, :nb], fin[:nb], norm_g, fg, ffn_w, gla_p, mla_p)
    y_sample = _trunk(x_sample, mods[:, nb:], fin[nb:], norm_g, fg, ffn_w, gla_p, mla_p)
    return (y_prompt, y_sample)
```

```python
import functools

import jax
import jax.numpy as jnp
from jax import lax
from jax.experimental import pallas as pl
from jax.experimental.pallas import tpu as pltpu

F32 = jnp.float32
BF16 = jnp.bfloat16

D_MODEL = 1024
DEPTH = 4
D_FF = 2816
N_MOD = 9
EPS = 1e-6
RES_HALF = 0.5

GLA_HEADS = 4
GLA_DK = 128
GLA_DV = 256
GLA_GATE_RANK = 16
GLA_TAU = 16.0
GLA_CHUNK = 64
GLA_QK = GLA_HEADS * GLA_DK
GLA_VR = GLA_HEADS * GLA_DV

MLA_HEADS = 8
MLA_NOPE = 128
MLA_ROPE = 64
MLA_V = 128
MLA_Q_RANK = 256
MLA_KV_RANK = 128
ROPE_THETA = 10000.0
MLA_QK_PAD = 256
MLA_HQ = MLA_HEADS * MLA_QK_PAD
MLA_HV = MLA_HEADS * MLA_V
LOG2_E = 1.4426950408889634
MLA_TK = 512
MLA_QG = 256

LANES = 128
VMEM_LIMIT = 56 * 1024 * 1024
ROW_TILE = 512


def _cparams(n_grid):
    return pltpu.CompilerParams(
        dimension_semantics=("arbitrary",) * n_grid,
        vmem_limit_bytes=VMEM_LIMIT)


def _silu(x):
    return x / (1.0 + jnp.exp(-x))


def _rms(x, g):
    ms = jnp.mean(x * x, axis=-1, keepdims=True)
    return x * lax.rsqrt(ms + EPS) * g


def _norm_mod(x, g, shift, scale):
    return _rms(x, g) * (1.0 + scale) + shift


def _dot(a, b):
    return jnp.dot(a, b, preferred_element_type=F32)


def _dot_nt(a, b):
    return lax.dot_general(a, b, (((1,), (1,)), ((), ())), preferred_element_type=F32)


def _dot_tn(a, b):
    return lax.dot_general(a, b, (((0,), (0,)), ((), ())), preferred_element_type=F32)


def _mod_kernel(c_ref, w_ref, b_ref, o_ref):
    c = _silu(c_ref[...]).astype(BF16)
    o_ref[...] = _dot(c, w_ref[...].astype(BF16)) + b_ref[...]


def _ada_proj(c, w, b, tn=1024):
    n_layers, d, n = w.shape
    bt = c.shape[0]
    return pl.pallas_call(
        _mod_kernel,
        grid=(n_layers, n // tn),
        in_specs=[
            pl.BlockSpec((bt, d), lambda l, j: (0, 0)),
            pl.BlockSpec((None, d, tn), lambda l, j: (l, 0, j)),
            pl.BlockSpec((None, 1, tn), lambda l, j: (l, 0, j)),
        ],
        out_specs=pl.BlockSpec((None, bt, tn), lambda l, j: (l, 0, j)),
        out_shape=jax.ShapeDtypeStruct((n_layers, bt, n), F32),
        compiler_params=_cparams(2),
        name="ada_proj",
    )(c, w, b.reshape(n_layers, 1, n))


def _gla_out_residual(x, of_ref, ob_ref, r_ref, gm_ref, gn_ref, w_ref):
    o = of_ref[...] + ob_ref[...]
    gn = gn_ref[...]
    normed = [_rms(o[:, h * GLA_DV:(h + 1) * GLA_DV], gn) for h in range(GLA_HEADS)]
    z = (_silu(r_ref[...]) * jnp.concatenate(normed, axis=1)).astype(BF16)
    return x + gm_ref[...] * _dot(z, w_ref[...])


def _mla_out_residual(x, ao_ref, gm_ref, w_ref):
    return x + gm_ref[...] * _dot(ao_ref[...], w_ref[...])


def _ffn_half_step(x, sh_ref, sc_ref, gt_ref, g_ref, wg_ref, wu_ref, wd_ref):
    h = _norm_mod(x, g_ref[...], sh_ref[...], sc_ref[...]).astype(BF16)
    gate = _dot(h, wg_ref[...])
    up = _dot(h, wu_ref[...])
    a = (_silu(gate) * up).astype(BF16)
    return x + (RES_HALF * gt_ref[...]) * _dot(a, wd_ref[...])


def _gla_in_proj(x, sh_ref, sc_ref, g_ref, w_ref, wa_ref, q_ref, k_ref, v_ref, r_ref, a_ref):
    h = _norm_mod(x, g_ref[...], sh_ref[...], sc_ref[...]).astype(BF16)
    q_ref[...] = _dot(h, w_ref[:, 0:GLA_QK]) * (GLA_DK ** -0.5)
    k_ref[...] = _dot(h, w_ref[:, GLA_QK:2 * GLA_QK])
    v_ref[...] = _dot(h, w_ref[:, 2 * GLA_QK:2 * GLA_QK + GLA_VR]).astype(BF16)
    r_ref[...] = _dot(h, w_ref[:, 2 * GLA_QK + GLA_VR:])
    a_ref[...] = _dot(h, wa_ref[...]).astype(BF16)


def _rope(x, cos, sin_lo, sin_hi):
    half = MLA_ROPE // 2
    return (x * cos + pltpu.roll(x, LANES - half, 1) * sin_lo
            + pltpu.roll(x, half, 1) * sin_hi)


def _mla_in_proj(x, sh_ref, sc_ref, g_ref, wq_ref, wkv_ref, wkr_ref, gq_ref, gkv_ref,
                 wuq_ref, wuk_ref, wuvt_ref, cos_ref, slo_ref, shi_ref, q_ref, k_ref, vt_ref):
    h = _norm_mod(x, g_ref[...], sh_ref[...], sc_ref[...]).astype(BF16)
    c_q = _rms(_dot(h, wq_ref[...]), gq_ref[...]).astype(BF16)
    c_kv = _rms(_dot(h, wkv_ref[...]), gkv_ref[...]).astype(BF16)
    cos, slo, shi = cos_ref[...], slo_ref[...], shi_ref[...]
    k_rope = _rope(_dot(h, wkr_ref[...]), cos, slo, shi).astype(BF16)
    scale = (MLA_NOPE + MLA_ROPE) ** -0.5 * LOG2_E
    vt = _dot_nt(wuvt_ref[...], c_kv).astype(BF16)
    for c in range(vt_ref.shape[0]):
        vt_ref[c] = vt[:, c * MLA_TK:(c + 1) * MLA_TK]
    for hd in range(MLA_HEADS):
        lo = hd * MLA_QK_PAD
        q_h = _dot(c_q, wuq_ref[:, lo:lo + MLA_QK_PAD])
        q_ref[:, lo:lo + LANES] = (q_h[:, :LANES] * scale).astype(BF16)
        q_ref[:, lo + LANES:lo + MLA_QK_PAD] = (
            _rope(q_h[:, LANES:], cos, slo, shi) * scale).astype(BF16)
        k_ref[:, lo:lo + LANES] = _dot(
            c_kv, wuk_ref[:, hd * MLA_NOPE:(hd + 1) * MLA_NOPE]).astype(BF16)
        k_ref[:, lo + LANES:lo + MLA_QK_PAD] = k_rope


_N_PRE = {None: 0, "gla": 6, "mla": 3}
_N_POST_IN = {None: 0, "final": 3, "gla": 5, "mla": 14}
_N_FFN = 7


def _sublayer_kernel(*refs, pre, post):
    n0 = 1 + _N_PRE[pre]
    n1 = n0 + _N_FFN
    n2 = n1 + _N_POST_IN[post]
    x = refs[0][...]
    if pre == "gla":
        x = _gla_out_residual(x, *refs[1:n0])
    elif pre == "mla":
        x = _mla_out_residual(x, *refs[1:n0])
    y = _ffn_half_step(x, *refs[n0:n1])
    if post == "final":
        fsh_ref, fsc_ref, fg_ref = refs[n1:n2]
        y = _norm_mod(y, fg_ref[...], fsh_ref[...], fsc_ref[...])
    refs[n2][...] = y
    if post == "gla":
        _gla_in_proj(y, *refs[n1:n2], *refs[n2 + 1:])
    elif post == "mla":
        _mla_in_proj(y, *refs[n1:n2], *refs[n2 + 1:])


def _row_spec(tm, d):
    return pl.BlockSpec((None, tm, d), lambda b, i: (b, i, 0))


def _vec_spec(d):
    return pl.BlockSpec((None, 1, d), lambda b, i: (b, 0, 0))


def _const_spec(shape):
    nd = len(shape)
    return pl.BlockSpec(shape, lambda b, i: (0,) * nd, pipeline_mode=pl.Buffered(1))


def _const_specs(arrays):
    return [_const_spec(a.shape) for a in arrays]


def _sublayer(x, ffn_args, pre=None, pre_args=(), post=None, post_args=()):
    bsz, s, d = x.shape
    tm = min(ROW_TILE, s)
    row = functools.partial(_row_spec, tm)
    vec3 = [_vec_spec(d)] * 3
    args, specs = [x], [row(d)]
    if pre == "gla":
        args += list(pre_args)
        specs += [row(GLA_VR)] * 3 + [_vec_spec(d)] + _const_specs(pre_args[4:])
    elif pre == "mla":
        args += list(pre_args)
        specs += [row(MLA_HV), _vec_spec(d)] + _const_specs(pre_args[2:])
    args += list(ffn_args)
    specs += vec3 + _const_specs(ffn_args[3:])
    args += list(post_args)
    x_out = jax.ShapeDtypeStruct(x.shape, F32)
    if post is None:
        out_shape, out_specs = x_out, row(d)
    elif post == "final":
        specs += [_vec_spec(d)] * 2 + _const_specs(post_args[2:])
        out_shape, out_specs = x_out, row(d)
    elif post == "gla":
        specs += [_vec_spec(d)] * 2 + _const_specs(post_args[2:])
        out_shape = [x_out,
                     jax.ShapeDtypeStruct((bsz, s, GLA_QK), F32),
                     jax.ShapeDtypeStruct((bsz, s, GLA_QK), F32),
                     jax.ShapeDtypeStruct((bsz, s, GLA_VR), BF16),
                     jax.ShapeDtypeStruct((bsz, s, GLA_VR), F32),
                     jax.ShapeDtypeStruct((bsz, s, LANES), BF16)]
        out_specs = [row(d), row(GLA_QK), row(GLA_QK), row(GLA_VR), row(GLA_VR), row(LANES)]
    else:
        assert tm % MLA_TK == 0
        tab = pl.BlockSpec((tm, LANES), lambda b, i: (i, 0))
        specs += [_vec_spec(d)] * 2 + _const_specs(post_args[2:11]) + [tab] * 3
        out_shape = [x_out,
                     jax.ShapeDtypeStruct((bsz, s, MLA_HQ), BF16),
                     jax.ShapeDtypeStruct((bsz, s, MLA_HQ), BF16),
                     jax.ShapeDtypeStruct((bsz, s // MLA_TK, MLA_HV, MLA_TK), BF16)]
        out_specs = [row(d), row(MLA_HQ), row(MLA_HQ),
                     pl.BlockSpec((None, tm // MLA_TK, MLA_HV, MLA_TK),
                                  lambda b, i: (b, i, 0, 0))]
    assert len(args) == 1 + _N_PRE[pre] + _N_FFN + _N_POST_IN[post]
    return pl.pallas_call(
        functools.partial(_sublayer_kernel, pre=pre, post=post),
        grid=(bsz, s // tm),
        in_specs=specs,
        out_specs=out_specs,
        out_shape=out_shape,
        compiler_params=_cparams(2),
        name="sublayer",
    )(*args)


def _log_sigmoid(x):
    return jnp.minimum(x, 0.0) - jnp.log(1.0 + jnp.exp(-jnp.abs(x)))


def _chunk_cumsum(tri, la):
    hi = la.astype(BF16)
    lo = (la - hi.astype(F32)).astype(BF16)
    return _dot(tri, hi) + _dot(tri, lo)


def _gla_direction(q_ref, k_ref, v_ref, la, tri, keep, last_row, state_ref, o_ref, order):
    c = GLA_CHUNK
    rows = [slice(i * c, (i + 1) * c) for i in order]
    b = [_chunk_cumsum(tri, la[r, :]) for r in rows]
    eb = [jnp.exp(x) for x in b]
    q_in = [(q_ref[r, :] * e).astype(BF16) for r, e in zip(rows, eb)]
    k_dec = [k_ref[r, :] * jnp.exp(-x) for r, x in zip(rows, b)]
    dec = [e[last_row:last_row + 1, :] for e in eb]
    k_out = [(kd * d).astype(BF16) for kd, d in zip(k_dec, dec)]
    a = [jnp.where(keep, _dot_nt(qi, kd.astype(BF16)), 0.0).astype(BF16)
         for qi, kd in zip(q_in, k_dec)]
    o_intra = [_dot(ai, v_ref[r, :]) for ai, r in zip(a, rows)]
    kv = [_dot_tn(v_ref[r, :], ko) for r, ko in zip(rows, k_out)]
    state = state_ref[...]
    for i, r in enumerate(rows):
        o_ref[r, :] = o_intra[i] + _dot_nt(q_in[i], state.astype(BF16))
        state = state * dec[i] + kv[i]
    state_ref[...] = state


def _gla_core_kernel(qf_ref, kf_ref, vf_ref, af_ref, qb_ref, kb_ref, vb_ref, ab_ref,
                     wf_ref, wb_ref, bg_ref, of_ref, ob_ref, sf_ref, sb_ref, *, n_chunks):
    @pl.when(pl.program_id(2) == 0)
    def _():
        sf_ref[...] = jnp.zeros_like(sf_ref)
        sb_ref[...] = jnp.zeros_like(sb_ref)

    c = GLA_CHUNK
    la_f = _log_sigmoid(_dot(af_ref[...], wf_ref[...]) + bg_ref[0:1, :]) / GLA_TAU
    la_b = _log_sigmoid(_dot(ab_ref[...], wb_ref[...]) + bg_ref[1:2, :]) / GLA_TAU
    row = lax.broadcasted_iota(jnp.int32, (c, c), 0)
    col = lax.broadcasted_iota(jnp.int32, (c, c), 1)
    lower = col <= row
    upper = col >= row
    tri_f = lower.astype(BF16)
    tri_b = upper.astype(BF16)
    _gla_direction(qf_ref, kf_ref, vf_ref, la_f, tri_f, lower, c - 1, sf_ref, of_ref,
                   range(n_chunks))
    _gla_direction(qb_ref, kb_ref, vb_ref, la_b, tri_b, upper, 0, sb_ref, ob_ref,
                   range(n_chunks - 1, -1, -1))


def _gla_core(q, k, v, a, w_up_f, w_up_b, b_gate, ts):
    bsz, s, _ = q.shape
    nt = s // ts
    fwd = lambda b, h, t: (b, t, h)
    bwd = lambda b, h, t: (b, nt - 1 - t, h)
    fwd0 = lambda b, h, t: (b, t, 0)
    bwd0 = lambda b, h, t: (b, nt - 1 - t, 0)
    head = lambda b, h, t: (0, h)
    qk = lambda m: pl.BlockSpec((None, ts, GLA_DK), m)
    vv = lambda m: pl.BlockSpec((None, ts, GLA_DV), m)
    return pl.pallas_call(
        functools.partial(_gla_core_kernel, n_chunks=ts // GLA_CHUNK),
        grid=(bsz, GLA_HEADS, nt),
        in_specs=[
            qk(fwd), qk(fwd), vv(fwd), pl.BlockSpec((None, ts, LANES), fwd0),
            qk(bwd), qk(bwd), vv(bwd), pl.BlockSpec((None, ts, LANES), bwd0),
            pl.BlockSpec((LANES, GLA_DK), head), pl.BlockSpec((LANES, GLA_DK), head),
            pl.BlockSpec((2, GLA_DK), head),
        ],
        out_specs=[vv(fwd), vv(bwd)],
        out_shape=[jax.ShapeDtypeStruct((bsz, s, GLA_VR), F32)] * 2,
        scratch_shapes=[pltpu.VMEM((GLA_DV, GLA_DK), F32)] * 2,
        compiler_params=_cparams(3),
        name="gla_core",
    )(q, k, v, a, q, k, v, a, w_up_f, w_up_b, b_gate)


def _mla_attn_kernel(q_ref, k_ref, vt_ref, o_ref, sa_ref, sb_ref, *, tk, n_kv):
    tq = q_ref.shape[0]
    gw = MLA_QG
    n_g = tq // gw
    cols = [slice(g * gw, (g + 1) * gw) for g in range(n_g)]
    qs = [q_ref[c, :] for c in cols]

    def scores(j, dst_ref, g, m):
        start = pl.multiple_of(j * tk, tk)
        s = _dot_nt(k_ref[pl.ds(start, tk), :], qs[g])
        dst_ref[:, cols[g]] = s
        return jnp.maximum(m, jnp.max(s, axis=0, keepdims=True))

    def probs(src_ref, g, m_prev, m_cur, l):
        alpha = jnp.exp2(m_prev - m_cur)
        p = jnp.exp2(src_ref[:, cols[g]] - m_cur)
        return alpha, p.astype(BF16), l * alpha + jnp.sum(p, axis=0, keepdims=True)

    def pair(i, carry, last):
        j = 2 * i
        st = [dict(zip(("m_prev", "m_cur", "l", "acc"), c)) for c in carry]

        def run(g, op):
            d = st[g]
            if op == 0:
                d["m_1"] = scores(j + 1, sb_ref, g, d["m_cur"])
            elif op == 1:
                d["alpha"], d["p"], d["l"] = probs(sa_ref, g, d["m_prev"], d["m_cur"], d["l"])
            elif op == 2:
                d["acc"] = d["acc"] * d["alpha"] + _dot(vt_ref[j], d["p"])
            elif op == 3:
                d["m_2"] = d["m_1"] if last else scores(j + 2, sa_ref, g, d["m_1"])
            elif op == 4:
                d["alpha"], d["p"], d["l"] = probs(sb_ref, g, d["m_cur"], d["m_1"], d["l"])
            else:
                d["acc"] = d["acc"] * d["alpha"] + _dot(vt_ref[j + 1], d["p"])

        n_ops = 6
        for t in range(n_ops + n_g - 1):
            for g in range(n_g):
                if 0 <= t - g < n_ops:
                    run(g, t - g)
        return tuple((d["m_1"], d["m_2"], d["l"], d["acc"]) for d in st)

    neg = jnp.full((1, gw), -jnp.inf, F32)
    init = tuple((neg, scores(0, sa_ref, g, neg), jnp.zeros((1, gw), F32),
                  jnp.zeros((MLA_V, gw), F32)) for g in range(n_g))
    carry = lax.fori_loop(0, n_kv // 2 - 1, functools.partial(pair, last=False), init)
    final = pair(n_kv // 2 - 1, carry, True)
    for g in range(n_g):
        _, _, l, acc = final[g]
        o_ref[cols[g], :] = (acc / l).T.astype(BF16)


def _mla_attn(q, k, vt, tq):
    bsz, s, _ = q.shape
    n_kv, tk = vt.shape[1], vt.shape[3]
    assert n_kv % 2 == 0
    return pl.pallas_call(
        functools.partial(_mla_attn_kernel, tk=tk, n_kv=n_kv),
        grid=(bsz, MLA_HEADS, s // tq),
        in_specs=[
            pl.BlockSpec((None, tq, MLA_QK_PAD), lambda b, h, i: (b, i, h)),
            pl.BlockSpec((None, s, MLA_QK_PAD), lambda b, h, i: (b, 0, h)),
            pl.BlockSpec((None, n_kv, MLA_V, tk), lambda b, h, i: (b, 0, h, 0)),
        ],
        out_specs=pl.BlockSpec((None, tq, MLA_V), lambda b, h, i: (b, i, h)),
        out_shape=jax.ShapeDtypeStruct((bsz, s, MLA_HV), BF16),
        scratch_shapes=[pltpu.VMEM((tk, tq), F32)] * 2,
        compiler_params=_cparams(3),
        name="mla_attn",
    )(q, k, vt)


def _rope_tables(s):
    half = MLA_ROPE // 2
    inv = 1.0 / (ROPE_THETA ** (jnp.arange(0, MLA_ROPE, 2, dtype=F32) / MLA_ROPE))
    ang = jnp.arange(s, dtype=F32)[:, None] * inv[None, :]
    cos, sin = jnp.cos(ang), jnp.sin(ang)
    z = jnp.zeros((s, half), F32)
    z2 = jnp.zeros((s, LANES - MLA_ROPE), F32)
    return (jnp.concatenate([cos, cos, z2], axis=1),
            jnp.concatenate([-sin, z, z2], axis=1),
            jnp.concatenate([z, sin, z2], axis=1))


def _prep_gla(w_in, w_gate_up, b_gate, g_norm, w_out):
    n_main = 2 * GLA_QK + 2 * GLA_VR
    w_a = jnp.pad(w_in[:, n_main:], ((0, 0), (0, LANES - 2 * GLA_GATE_RANK)))
    r = GLA_GATE_RANK
    w_up_f = jnp.pad(w_gate_up[0], ((0, LANES - r), (0, 0)))
    w_up_b = jnp.pad(w_gate_up[1], ((r, LANES - 2 * r), (0, 0)))
    return dict(w_qkvr=w_in[:, :n_main].astype(BF16), w_a=w_a.astype(BF16),
                w_up_f=w_up_f.astype(BF16), w_up_b=w_up_b.astype(BF16),
                b_gate=b_gate, g_norm=g_norm.reshape(1, GLA_DV), w_out=w_out.astype(BF16))


def _prep_mla(w_in, g_q, g_kv, w_uq, w_ukv, w_out):
    w_kr = jnp.pad(w_in[:, MLA_Q_RANK + MLA_KV_RANK:], ((0, 0), (0, LANES - MLA_ROPE)))
    uq = w_uq.reshape(MLA_Q_RANK, MLA_HEADS, MLA_NOPE + MLA_ROPE)
    uq = jnp.pad(uq, ((0, 0), (0, 0), (0, MLA_QK_PAD - MLA_NOPE - MLA_ROPE)))
    ukv = w_ukv.reshape(MLA_KV_RANK, MLA_HEADS, MLA_NOPE + MLA_V)
    return dict(
        w_cq=w_in[:, :MLA_Q_RANK].astype(BF16),
        w_ckv=w_in[:, MLA_Q_RANK:MLA_Q_RANK + MLA_KV_RANK].astype(BF16),
        w_kr=w_kr.astype(BF16),
        g_q=g_q.reshape(1, MLA_Q_RANK), g_kv=g_kv.reshape(1, MLA_KV_RANK),
        w_uq=uq.reshape(MLA_Q_RANK, MLA_HQ).astype(BF16),
        w_uk=ukv[:, :, :MLA_NOPE].reshape(MLA_KV_RANK, MLA_HEADS * MLA_NOPE).astype(BF16),
        w_uvt=ukv[:, :, MLA_NOPE:].reshape(MLA_KV_RANK, MLA_HV).T.astype(BF16),
        w_out=w_out.astype(BF16))


def _trunk(x, mods, fin, norm_g, final_g, ffn_w, gla_p, mla_p):
    bsz, s, d = x.shape
    rope_tabs = _rope_tables(s)
    vec = lambda m: m.reshape(bsz, 1, d)
    wg, wu, wd = ffn_w
    for i in range(DEPTH):
        s1, sc1, g1, sm, scm, gm, s2, sc2, g2 = [
            vec(mods[i, :, j * d:(j + 1) * d]) for j in range(N_MOD)]
        ffn1 = (s1, sc1, g1, norm_g[i, 0][None], wg[i, 0], wu[i, 0], wd[i, 0])
        ffn2 = (s2, sc2, g2, norm_g[i, 2][None], wg[i, 1], wu[i, 1], wd[i, 1])
        j = i // 2
        if i % 2 == 0:
            p = gla_p[j]
            x, q, k, v, r, a = _sublayer(
                x, ffn1, post="gla",
                post_args=(sm, scm, norm_g[i, 1][None], p["w_qkvr"], p["w_a"]))
            o_f, o_b = _gla_core(q, k, v, a, p["w_up_f"], p["w_up_b"], p["b_gate"],
                                 min(512, s))
            pre, pre_args = "gla", (o_f, o_b, r, gm, p["g_norm"], p["w_out"])
        else:
            p = mla_p[j]
            x, q, k, vt = _sublayer(
                x, ffn1, post="mla",
                post_args=(sm, scm, norm_g[i, 1][None], p["w_cq"], p["w_ckv"], p["w_kr"],
                           p["g_q"], p["g_kv"], p["w_uq"], p["w_uk"], p["w_uvt"]) + rope_tabs)
            pre, pre_args = "mla", (_mla_attn(q, k, vt, min(1024, s)), gm, p["w_out"])
        if i == DEPTH - 1:
            x = _sublayer(x, ffn2, pre, pre_args, "final",
                          (vec(fin[:, :d]), vec(fin[:, d:]), final_g))
        else:
            x = _sublayer(x, ffn2, pre, pre_args)
    return x


def kernel(x_prompt, x_sample, c_prompt, c_sample, ada_w, ada_b, norm_g, ffn_w_gate, ffn_w_up, ffn_w_down, gla_w_in, gla_w_gate_up, gla_b_gate, gla_g_norm, gla_w_out, mla_w_in, mla_g_q, mla_g_kv, mla_w_uq, mla_w_ukv, mla_w_out, final_ada_w, final_ada_b, final_g):
    nb = x_prompt.shape[0]
    c_all = jnp.concatenate([c_prompt, c_sample], axis=0)
    mods = _ada_proj(c_all, ada_w, ada_b)
    fin = _ada_proj(c_all, final_ada_w[None], final_ada_b[None])[0]
    ffn_w = (ffn_w_gate.astype(BF16), ffn_w_up.astype(BF16), ffn_w_down.astype(BF16))
    gla_p = [_prep_gla(gla_w_in[j], gla_w_gate_up[j], gla_b_gate[j], gla_g_norm[j],
                       gla_w_out[j]) for j in range(gla_w_in.shape[0])]
    mla_p = [_prep_mla(mla_w_in[j], mla_g_q[j], mla_g_kv[j], mla_w_uq[j], mla_w_ukv[j],
                       mla_w_out[j]) for j in range(mla_w_in.shape[0])]
    fg = final_g.reshape(1, -1)
    y_prompt = _trunk(x_prompt, mods[:, :nb], fin[:nb], norm_g, fg, ffn_w, gla_p, mla_p)
    y_sample = _trunk(x_sample, mods[:, nb:], fin[nb:], norm_g, fg, ffn_w, gla_p, mla_p)
    return (y_prompt, y_sample)
```

```python
import functools

import jax
import jax.numpy as jnp
from jax import lax
from jax.experimental import pallas as pl
from jax.experimental.pallas import tpu as pltpu

F32 = jnp.float32
BF16 = jnp.bfloat16

D_MODEL = 1024
DEPTH = 4
D_FF = 2816
N_MOD = 9
EPS = 1e-6
RES_HALF = 0.5

GLA_HEADS = 4
GLA_DK = 128
GLA_DV = 256
GLA_GATE_RANK = 16
GLA_TAU = 16.0
GLA_CHUNK = 64
GLA_QK = GLA_HEADS * GLA_DK
GLA_VR = GLA_HEADS * GLA_DV

MLA_HEADS = 8
MLA_NOPE = 128
MLA_ROPE = 64
MLA_V = 128
MLA_Q_RANK = 256
MLA_KV_RANK = 128
ROPE_THETA = 10000.0
MLA_QK_PAD = 256
MLA_HQ = MLA_HEADS * MLA_QK_PAD
MLA_HV = MLA_HEADS * MLA_V
LOG2_E = 1.4426950408889634
MLA_TK = 512
MLA_QG = 256

LANES = 128
VMEM_LIMIT = 56 * 1024 * 1024
ROW_TILE = 512


def _cparams(n_grid):
    return pltpu.CompilerParams(
        dimension_semantics=("arbitrary",) * n_grid,
        vmem_limit_bytes=VMEM_LIMIT)


def _silu(x):
    return x / (1.0 + jnp.exp(-x))


def _rms(x, g):
    ms = jnp.mean(x * x, axis=-1, keepdims=True)
    return x * lax.rsqrt(ms + EPS) * g


def _norm_mod(x, g, shift, scale):
    return _rms(x, g) * (1.0 + scale) + shift


def _dot(a, b):
    return jnp.dot(a, b, preferred_element_type=F32)


def _dot_nt(a, b):
    return lax.dot_general(a, b, (((1,), (1,)), ((), ())), preferred_element_type=F32)


def _dot_tn(a, b):
    return lax.dot_general(a, b, (((0,), (0,)), ((), ())), preferred_element_type=F32)


def _mod_kernel(c_ref, w_ref, b_ref, o_ref):
    c = _silu(c_ref[...]).astype(BF16)
    o_ref[...] = _dot(c, w_ref[...].astype(BF16)) + b_ref[...]


def _ada_proj(c, w, b, tn=1024):
    n_layers, d, n = w.shape
    bt = c.shape[0]
    return pl.pallas_call(
        _mod_kernel,
        grid=(n_layers, n // tn),
        in_specs=[
            pl.BlockSpec((bt, d), lambda l, j: (0, 0)),
            pl.BlockSpec((None, d, tn), lambda l, j: (l, 0, j)),
            pl.BlockSpec((None, 1, tn), lambda l, j: (l, 0, j)),
        ],
        out_specs=pl.BlockSpec((None, bt, tn), lambda l, j: (l, 0, j)),
        out_shape=jax.ShapeDtypeStruct((n_layers, bt, n), F32),
        compiler_params=_cparams(2),
        name="ada_proj",
    )(c, w, b.reshape(n_layers, 1, n))


def _gla_out_residual(x, of_ref, ob_ref, r_ref, gm_ref, gn_ref, w_ref):
    o = of_ref[...] + ob_ref[...]
    gn = gn_ref[...]
    normed = [_rms(o[:, h * GLA_DV:(h + 1) * GLA_DV], gn) for h in range(GLA_HEADS)]
    z = (_silu(r_ref[...]) * jnp.concatenate(normed, axis=1)).astype(BF16)
    return x + gm_ref[...] * _dot(z, w_ref[...])


def _mla_out_residual(x, ao_ref, gm_ref, w_ref):
    return x + gm_ref[...] * _dot(ao_ref[...], w_ref[...])


def _gla_in_proj(x, sh_ref, sc_ref, g_ref, w_ref, wa_ref, q_ref, k_ref, v_ref, r_ref, a_ref):
    h = _norm_mod(x, g_ref[...], sh_ref[...], sc_ref[...]).astype(BF16)
    q_ref[...] = _dot(h, w_ref[:, 0:GLA_QK]) * (GLA_DK ** -0.5)
    k_ref[...] = _dot(h, w_ref[:, GLA_QK:2 * GLA_QK])
    v_ref[...] = _dot(h, w_ref[:, 2 * GLA_QK:2 * GLA_QK + GLA_VR]).astype(BF16)
    r_ref[...] = _dot(h, w_ref[:, 2 * GLA_QK + GLA_VR:])
    a_ref[...] = _dot(h, wa_ref[...]).astype(BF16)


def _rope(x, cos, sin_lo, sin_hi):
    half = MLA_ROPE // 2
    return (x * cos + pltpu.roll(x, LANES - half, 1) * sin_lo
            + pltpu.roll(x, half, 1) * sin_hi)


def _mla_in_proj(x, sh_ref, sc_ref, g_ref, wq_ref, wkv_ref, wkr_ref, gq_ref, gkv_ref,
                 wuq_ref, wuk_ref, wuvt_ref, cos_ref, slo_ref, shi_ref, q_ref, k_ref, vt_ref):
    h = _norm_mod(x, g_ref[...], sh_ref[...], sc_ref[...]).astype(BF16)
    c_q = _rms(_dot(h, wq_ref[...]), gq_ref[...]).astype(BF16)
    c_kv = _rms(_dot(h, wkv_ref[...]), gkv_ref[...]).astype(BF16)
    cos, slo, shi = cos_ref[...], slo_ref[...], shi_ref[...]
    k_rope = _rope(_dot(h, wkr_ref[...]), cos, slo, shi).astype(BF16)
    scale = (MLA_NOPE + MLA_ROPE) ** -0.5 * LOG2_E
    vt_ref[...] = _dot_nt(wuvt_ref[...], c_kv).astype(BF16)
    for hd in range(MLA_HEADS):
        lo = hd * MLA_QK_PAD
        q_h = _dot(c_q, wuq_ref[:, lo:lo + MLA_QK_PAD])
        q_ref[:, lo:lo + LANES] = (q_h[:, :LANES] * scale).astype(BF16)
        q_ref[:, lo + LANES:lo + MLA_QK_PAD] = (
            _rope(q_h[:, LANES:], cos, slo, shi) * scale).astype(BF16)
        k_ref[:, lo:lo + LANES] = _dot(
            c_kv, wuk_ref[:, hd * MLA_NOPE:(hd + 1) * MLA_NOPE]).astype(BF16)
        k_ref[:, lo + LANES:lo + MLA_QK_PAD] = k_rope


_N_PRE = {None: 0, "gla": 6, "mla": 3}
_N_POST_IN = {None: 0, "final": 3, "gla": 5, "mla": 14}
_N_FFN = 7
SUB_ROWS = 256


def _part_stages(refs, pre, post, row0, nrows):
    n0 = 1 + _N_PRE[pre]
    n1 = n0 + _N_FFN
    n2 = n1 + _N_POST_IN[post]
    rows = pl.ds(row0, nrows)
    rv = lambda r: r.at[rows]
    sh_ref, sc_ref, gt_ref, g_ref, wg_ref, wu_ref, wd_ref = refs[n0:n1]
    v = {}

    def load():
        x = refs[0][rows, :]
        if pre == "gla":
            of_ref, ob_ref, r_ref = (rv(r) for r in refs[1:4])
            x = _gla_out_residual(x, of_ref, ob_ref, r_ref, *refs[4:n0])
        elif pre == "mla":
            x = _mla_out_residual(x, rv(refs[1]), *refs[2:n0])
        v["x"] = x

    def norm():
        v["h"] = _norm_mod(v["x"], g_ref[...], sh_ref[...], sc_ref[...]).astype(BF16)

    def up():
        v["gate"] = _dot(v["h"], wg_ref[...])
        v["up"] = _dot(v["h"], wu_ref[...])

    def act():
        v["a"] = (_silu(v["gate"]) * v["up"]).astype(BF16)

    def down():
        y = v["x"] + (RES_HALF * gt_ref[...]) * _dot(v["a"], wd_ref[...])
        if post == "final":
            fsh_ref, fsc_ref, fg_ref = refs[n1:n2]
            y = _norm_mod(y, fg_ref[...], fsh_ref[...], fsc_ref[...])
        refs[n2][rows, :] = y
        v["y"] = y

    def proj():
        if post == "gla":
            _gla_in_proj(v["y"], *refs[n1:n2], *(rv(r) for r in refs[n2 + 1:]))
        elif post == "mla":
            q_ref, k_ref, vt_ref = refs[n2 + 1:]
            chunk, off = divmod(row0, MLA_TK)
            ins = list(refs[n1:n2 - 3]) + [rv(r) for r in refs[n2 - 3:n2]]
            _mla_in_proj(v["y"], *ins, rv(q_ref), rv(k_ref),
                         vt_ref.at[chunk, :, pl.ds(off, nrows)])

    return [load, norm, up, act, down, proj]


def _sublayer_kernel(*refs, pre, post):
    tm = refs[0].shape[0]
    nrows = min(SUB_ROWS, tm)
    parts = [_part_stages(refs, pre, post, r0, nrows) for r0 in range(0, tm, nrows)]
    n_st = len(parts[0])
    lag = 1
    order = []
    for t in range(n_st + lag * (len(parts) - 1)):
        for pi in reversed(range(len(parts))):
            if 0 <= t - lag * pi < n_st:
                order.append(parts[pi][t - lag * pi])
    for fn in order:
        fn()


def _row_spec(tm, d):
    return pl.BlockSpec((None, tm, d), lambda b, i: (b, i, 0))


def _vec_spec(d):
    return pl.BlockSpec((None, 1, d), lambda b, i: (b, 0, 0))


def _const_spec(shape):
    nd = len(shape)
    return pl.BlockSpec(shape, lambda b, i: (0,) * nd, pipeline_mode=pl.Buffered(1))


def _const_specs(arrays):
    return [_const_spec(a.shape) for a in arrays]


def _sublayer(x, ffn_args, pre=None, pre_args=(), post=None, post_args=()):
    bsz, s, d = x.shape
    tm = min(ROW_TILE, s)
    row = functools.partial(_row_spec, tm)
    vec3 = [_vec_spec(d)] * 3
    args, specs = [x], [row(d)]
    if pre == "gla":
        args += list(pre_args)
        specs += [row(GLA_VR)] * 3 + [_vec_spec(d)] + _const_specs(pre_args[4:])
    elif pre == "mla":
        args += list(pre_args)
        specs += [row(MLA_HV), _vec_spec(d)] + _const_specs(pre_args[2:])
    args += list(ffn_args)
    specs += vec3 + _const_specs(ffn_args[3:])
    args += list(post_args)
    x_out = jax.ShapeDtypeStruct(x.shape, F32)
    if post is None:
        out_shape, out_specs = x_out, row(d)
    elif post == "final":
        specs += [_vec_spec(d)] * 2 + _const_specs(post_args[2:])
        out_shape, out_specs = x_out, row(d)
    elif post == "gla":
        specs += [_vec_spec(d)] * 2 + _const_specs(post_args[2:])
        out_shape = [x_out,
                     jax.ShapeDtypeStruct((bsz, s, GLA_QK), F32),
                     jax.ShapeDtypeStruct((bsz, s, GLA_QK), F32),
                     jax.ShapeDtypeStruct((bsz, s, GLA_VR), BF16),
                     jax.ShapeDtypeStruct((bsz, s, GLA_VR), F32),
                     jax.ShapeDtypeStruct((bsz, s, LANES), BF16)]
        out_specs = [row(d), row(GLA_QK), row(GLA_QK), row(GLA_VR), row(GLA_VR), row(LANES)]
    else:
        assert tm % MLA_TK == 0
        tab = pl.BlockSpec((tm, LANES), lambda b, i: (i, 0))
        specs += [_vec_spec(d)] * 2 + _const_specs(post_args[2:11]) + [tab] * 3
        out_shape = [x_out,
                     jax.ShapeDtypeStruct((bsz, s, MLA_HQ), BF16),
                     jax.ShapeDtypeStruct((bsz, s, MLA_HQ), BF16),
                     jax.ShapeDtypeStruct((bsz, s // MLA_TK, MLA_HV, MLA_TK), BF16)]
        out_specs = [row(d), row(MLA_HQ), row(MLA_HQ),
                     pl.BlockSpec((None, tm // MLA_TK, MLA_HV, MLA_TK),
                                  lambda b, i: (b, i, 0, 0))]
    assert len(args) == 1 + _N_PRE[pre] + _N_FFN + _N_POST_IN[post]
    return pl.pallas_call(
        functools.partial(_sublayer_kernel, pre=pre, post=post),
        grid=(bsz, s // tm),
        in_specs=specs,
        out_specs=out_specs,
        out_shape=out_shape,
        compiler_params=_cparams(2),
        name="sublayer",
    )(*args)


def _log_sigmoid(x):
    return jnp.minimum(x, 0.0) - jnp.log(1.0 + jnp.exp(-jnp.abs(x)))


def _chunk_cumsum(tri, la):
    hi = la.astype(BF16)
    lo = (la - hi.astype(F32)).astype(BF16)
    return _dot(tri, hi) + _dot(tri, lo)


def _gla_direction(q_ref, k_ref, v_ref, la, tri, keep, last_row, state_ref, o_ref, order):
    c = GLA_CHUNK
    rows = [slice(i * c, (i + 1) * c) for i in order]
    b = [_chunk_cumsum(tri, la[r, :]) for r in rows]
    eb = [jnp.exp(x) for x in b]
    q_in = [(q_ref[r, :] * e).astype(BF16) for r, e in zip(rows, eb)]
    k_dec = [k_ref[r, :] * jnp.exp(-x) for r, x in zip(rows, b)]
    dec = [e[last_row:last_row + 1, :] for e in eb]
    k_out = [(kd * d).astype(BF16) for kd, d in zip(k_dec, dec)]
    a = [jnp.where(keep, _dot_nt(qi, kd.astype(BF16)), 0.0).astype(BF16)
         for qi, kd in zip(q_in, k_dec)]
    o_intra = [_dot(ai, v_ref[r, :]) for ai, r in zip(a, rows)]
    kv = [_dot_tn(v_ref[r, :], ko) for r, ko in zip(rows, k_out)]
    state = state_ref[...]
    for i, r in enumerate(rows):
        o_ref[r, :] = o_intra[i] + _dot_nt(q_in[i], state.astype(BF16))
        state = state * dec[i] + kv[i]
    state_ref[...] = state


def _gla_core_kernel(qf_ref, kf_ref, vf_ref, af_ref, qb_ref, kb_ref, vb_ref, ab_ref,
                     wf_ref, wb_ref, bg_ref, of_ref, ob_ref, sf_ref, sb_ref, *, n_chunks):
    @pl.when(pl.program_id(2) == 0)
    def _():
        sf_ref[...] = jnp.zeros_like(sf_ref)
        sb_ref[...] = jnp.zeros_like(sb_ref)

    c = GLA_CHUNK
    la_f = _log_sigmoid(_dot(af_ref[...], wf_ref[...]) + bg_ref[0:1, :]) / GLA_TAU
    la_b = _log_sigmoid(_dot(ab_ref[...], wb_ref[...]) + bg_ref[1:2, :]) / GLA_TAU
    row = lax.broadcasted_iota(jnp.int32, (c, c), 0)
    col = lax.broadcasted_iota(jnp.int32, (c, c), 1)
    lower = col <= row
    upper = col >= row
    tri_f = lower.astype(BF16)
    tri_b = upper.astype(BF16)
    _gla_direction(qf_ref, kf_ref, vf_ref, la_f, tri_f, lower, c - 1, sf_ref, of_ref,
                   range(n_chunks))
    _gla_direction(qb_ref, kb_ref, vb_ref, la_b, tri_b, upper, 0, sb_ref, ob_ref,
                   range(n_chunks - 1, -1, -1))


def _gla_core(q, k, v, a, w_up_f, w_up_b, b_gate, ts):
    bsz, s, _ = q.shape
    nt = s // ts
    fwd = lambda b, h, t: (b, t, h)
    bwd = lambda b, h, t: (b, nt - 1 - t, h)
    fwd0 = lambda b, h, t: (b, t, 0)
    bwd0 = lambda b, h, t: (b, nt - 1 - t, 0)
    head = lambda b, h, t: (0, h)
    qk = lambda m: pl.BlockSpec((None, ts, GLA_DK), m)
    vv = lambda m: pl.BlockSpec((None, ts, GLA_DV), m)
    return pl.pallas_call(
        functools.partial(_gla_core_kernel, n_chunks=ts // GLA_CHUNK),
        grid=(bsz, GLA_HEADS, nt),
        in_specs=[
            qk(fwd), qk(fwd), vv(fwd), pl.BlockSpec((None, ts, LANES), fwd0),
            qk(bwd), qk(bwd), vv(bwd), pl.BlockSpec((None, ts, LANES), bwd0),
            pl.BlockSpec((LANES, GLA_DK), head), pl.BlockSpec((LANES, GLA_DK), head),
            pl.BlockSpec((2, GLA_DK), head),
        ],
        out_specs=[vv(fwd), vv(bwd)],
        out_shape=[jax.ShapeDtypeStruct((bsz, s, GLA_VR), F32)] * 2,
        scratch_shapes=[pltpu.VMEM((GLA_DV, GLA_DK), F32)] * 2,
        compiler_params=_cparams(3),
        name="gla_core",
    )(q, k, v, a, q, k, v, a, w_up_f, w_up_b, b_gate)


def _mla_attn_kernel(q_ref, k_ref, vt_ref, o_ref, sa_ref, sb_ref, *, tk, n_kv):
    tq = q_ref.shape[0]
    gw = MLA_QG
    n_g = tq // gw
    cols = [slice(g * gw, (g + 1) * gw) for g in range(n_g)]
    qs = [q_ref[c, :] for c in cols]

    def scores(j, dst_ref, g, m):
        start = pl.multiple_of(j * tk, tk)
        s = _dot_nt(k_ref[pl.ds(start, tk), :], qs[g])
        dst_ref[:, cols[g]] = s
        return jnp.maximum(m, jnp.max(s, axis=0, keepdims=True))

    def probs(src_ref, g, m_prev, m_cur, l):
        alpha = jnp.exp2(m_prev - m_cur)
        p = jnp.exp2(src_ref[:, cols[g]] - m_cur)
        return alpha, p.astype(BF16), l * alpha + jnp.sum(p, axis=0, keepdims=True)

    def pair(i, carry, last):
        j = 2 * i
        st = [dict(zip(("m_prev", "m_cur", "l", "acc"), c)) for c in carry]

        def run(g, op):
            d = st[g]
            if op == 0:
                d["m_1"] = scores(j + 1, sb_ref, g, d["m_cur"])
            elif op == 1:
                d["alpha"], d["p"], d["l"] = probs(sa_ref, g, d["m_prev"], d["m_cur"], d["l"])
            elif op == 2:
                d["acc"] = d["acc"] * d["alpha"] + _dot(vt_ref[j], d["p"])
            elif op == 3:
                d["m_2"] = d["m_1"] if last else scores(j + 2, sa_ref, g, d["m_1"])
            elif op == 4:
                d["alpha"], d["p"], d["l"] = probs(sb_ref, g, d["m_cur"], d["m_1"], d["l"])
            else:
                d["acc"] = d["acc"] * d["alpha"] + _dot(vt_ref[j + 1], d["p"])

        n_ops = 6
        for t in range(n_ops + n_g - 1):
            for g in range(n_g):
                if 0 <= t - g < n_ops:
                    run(g, t - g)
        return tuple((d["m_1"], d["m_2"], d["l"], d["acc"]) for d in st)

    neg = jnp.full((1, gw), -jnp.inf, F32)
    init = tuple((neg, scores(0, sa_ref, g, neg), jnp.zeros((1, gw), F32),
                  jnp.zeros((MLA_V, gw), F32)) for g in range(n_g))
    carry = lax.fori_loop(0, n_kv // 2 - 1, functools.partial(pair, last=False), init)
    final = pair(n_kv // 2 - 1, carry, True)
    for g in range(n_g):
        _, _, l, acc = final[g]
        o_ref[cols[g], :] = (acc / l).T.astype(BF16)


def _mla_attn(q, k, vt, tq):
    bsz, s, _ = q.shape
    n_kv, tk = vt.shape[1], vt.shape[3]
    assert n_kv % 2 == 0
    return pl.pallas_call(
        functools.partial(_mla_attn_kernel, tk=tk, n_kv=n_kv),
        grid=(bsz, MLA_HEADS, s // tq),
        in_specs=[
            pl.BlockSpec((None, tq, MLA_QK_PAD), lambda b, h, i: (b, i, h)),
            pl.BlockSpec((None, s, MLA_QK_PAD), lambda b, h, i: (b, 0, h)),
            pl.BlockSpec((None, n_kv, MLA_V, tk), lambda b, h, i: (b, 0, h, 0)),
        ],
        out_specs=pl.BlockSpec((None, tq, MLA_V), lambda b, h, i: (b, i, h)),
        out_shape=jax.ShapeDtypeStruct((bsz, s, MLA_HV), BF16),
        scratch_shapes=[pltpu.VMEM((tk, tq), F32)] * 2,
        compiler_params=_cparams(3),
        name="mla_attn",
    )(q, k, vt)


def _rope_tables(s):
    half = MLA_ROPE // 2
    inv = 1.0 / (ROPE_THETA ** (jnp.arange(0, MLA_ROPE, 2, dtype=F32) / MLA_ROPE))
    ang = jnp.arange(s, dtype=F32)[:, None] * inv[None, :]
    cos, sin = jnp.cos(ang), jnp.sin(ang)
    z = jnp.zeros((s, half), F32)
    z2 = jnp.zeros((s, LANES - MLA_ROPE), F32)
    return (jnp.concatenate([cos, cos, z2], axis=1),
            jnp.concatenate([-sin, z, z2], axis=1),
            jnp.concatenate([z, sin, z2], axis=1))


def _prep_gla(w_in, w_gate_up, b_gate, g_norm, w_out):
    n_main = 2 * GLA_QK + 2 * GLA_VR
    w_a = jnp.pad(w_in[:, n_main:], ((0, 0), (0, LANES - 2 * GLA_GATE_RANK)))
    r = GLA_GATE_RANK
    w_up_f = jnp.pad(w_gate_up[0], ((0, LANES - r), (0, 0)))
    w_up_b = jnp.pad(w_gate_up[1], ((r, LANES - 2 * r), (0, 0)))
    return dict(w_qkvr=w_in[:, :n_main].astype(BF16), w_a=w_a.astype(BF16),
                w_up_f=w_up_f.astype(BF16), w_up_b=w_up_b.astype(BF16),
                b_gate=b_gate, g_norm=g_norm.reshape(1, GLA_DV), w_out=w_out.astype(BF16))


def _prep_mla(w_in, g_q, g_kv, w_uq, w_ukv, w_out):
    w_kr = jnp.pad(w_in[:, MLA_Q_RANK + MLA_KV_RANK:], ((0, 0), (0, LANES - MLA_ROPE)))
    uq = w_uq.reshape(MLA_Q_RANK, MLA_HEADS, MLA_NOPE + MLA_ROPE)
    uq = jnp.pad(uq, ((0, 0), (0, 0), (0, MLA_QK_PAD - MLA_NOPE - MLA_ROPE)))
    ukv = w_ukv.reshape(MLA_KV_RANK, MLA_HEADS, MLA_NOPE + MLA_V)
    return dict(
        w_cq=w_in[:, :MLA_Q_RANK].astype(BF16),
        w_ckv=w_in[:, MLA_Q_RANK:MLA_Q_RANK + MLA_KV_RANK].astype(BF16),
        w_kr=w_kr.astype(BF16),
        g_q=g_q.reshape(1, MLA_Q_RANK), g_kv=g_kv.reshape(1, MLA_KV_RANK),
        w_uq=uq.reshape(MLA_Q_RANK, MLA_HQ).astype(BF16),
        w_uk=ukv[:, :, :MLA_NOPE].reshape(MLA_KV_RANK, MLA_HEADS * MLA_NOPE).astype(BF16),
        w_uvt=ukv[:, :, MLA_NOPE:].reshape(MLA_KV_RANK, MLA_HV).T.astype(BF16),
        w_out=w_out.astype(BF16))


def _trunk(x, mods, fin, norm_g, final_g, ffn_w, gla_p, mla_p):
    bsz, s, d = x.shape
    rope_tabs = _rope_tables(s)
    vec = lambda m: m.reshape(bsz, 1, d)
    wg, wu, wd = ffn_w
    for i in range(DEPTH):
        s1, sc1, g1, sm, scm, gm, s2, sc2, g2 = [
            vec(mods[i, :, j * d:(j + 1) * d]) for j in range(N_MOD)]
        ffn1 = (s1, sc1, g1, norm_g[i, 0][None], wg[i, 0], wu[i, 0], wd[i, 0])
        ffn2 = (s2, sc2, g2, norm_g[i, 2][None], wg[i, 1], wu[i, 1], wd[i, 1])
        j = i // 2
        if i % 2 == 0:
            p = gla_p[j]
            x, q, k, v, r, a = _sublayer(
                x, ffn1, post="gla",
                post_args=(sm, scm, norm_g[i, 1][None], p["w_qkvr"], p["w_a"]))
            o_f, o_b = _gla_core(q, k, v, a, p["w_up_f"], p["w_up_b"], p["b_gate"],
                                 min(512, s))
            pre, pre_args = "gla", (o_f, o_b, r, gm, p["g_norm"], p["w_out"])
        else:
            p = mla_p[j]
            x, q, k, vt = _sublayer(
                x, ffn1, post="mla",
                post_args=(sm, scm, norm_g[i, 1][None], p["w_cq"], p["w_ckv"], p["w_kr"],
                           p["g_q"], p["g_kv"], p["w_uq"], p["w_uk"], p["w_uvt"]) + rope_tabs)
            pre, pre_args = "mla", (_mla_attn(q, k, vt, min(2048, s)), gm, p["w_out"])
        if i == DEPTH - 1:
            x = _sublayer(x, ffn2, pre, pre_args, "final",
                          (vec(fin[:, :d]), vec(fin[:, d:]), final_g))
        else:
            x = _sublayer(x, ffn2, pre, pre_args)
    return x


def kernel(x_prompt, x_sample, c_prompt, c_sample, ada_w, ada_b, norm_g, ffn_w_gate, ffn_w_up, ffn_w_down, gla_w_in, gla_w_gate_up, gla_b_gate, gla_g_norm, gla_w_out, mla_w_in, mla_g_q, mla_g_kv, mla_w_uq, mla_w_ukv, mla_w_out, final_ada_w, final_ada_b, final_g):
    nb = x_prompt.shape[0]
    c_all = jnp.concatenate([c_prompt, c_sample], axis=0)
    mods = _ada_proj(c_all, ada_w, ada_b)
    fin = _ada_proj(c_all, final_ada_w[None], final_ada_b[None])[0]
    ffn_w = (ffn_w_gate.astype(BF16), ffn_w_up.astype(BF16), ffn_w_down.astype(BF16))
    gla_p = [_prep_gla(gla_w_in[j], gla_w_gate_up[j], gla_b_gate[j], gla_g_norm[j],
                       gla_w_out[j]) for j in range(gla_w_in.shape[0])]
    mla_p = [_prep_mla(mla_w_in[j], mla_g_q[j], mla_g_kv[j], mla_w_uq[j], mla_w_ukv[j],
                       mla_w_out[j]) for j in range(mla_w_in.shape[0])]
    fg = final_g.reshape(1, -1)
    y_prompt = _trunk(x_prompt, mods[:, :nb], fin[:nb], norm_g, fg, ffn_w, gla_p, mla_p)
    y_sample = _trunk(x_sample, mods[:, nb:], fin[nb:], norm_g, fg, ffn_w, gla_p, mla_p)
    return (y_prompt, y_sample)
```

```python
import functools

import jax
import jax.numpy as jnp
from jax import lax
from jax.experimental import pallas as pl
from jax.experimental.pallas import tpu as pltpu

F32 = jnp.float32
BF16 = jnp.bfloat16

D_MODEL = 1024
DEPTH = 4
D_FF = 2816
N_MOD = 9
EPS = 1e-6
RES_HALF = 0.5

GLA_HEADS = 4
GLA_DK = 128
GLA_DV = 256
GLA_GATE_RANK = 16
GLA_TAU = 16.0
GLA_CHUNK = 64
GLA_QK = GLA_HEADS * GLA_DK
GLA_VR = GLA_HEADS * GLA_DV
GLA_BLOCK = 1024

MLA_HEADS = 8
MLA_NOPE = 128
MLA_ROPE = 64
MLA_V = 128
MLA_Q_RANK = 256
MLA_KV_RANK = 128
ROPE_THETA = 10000.0
MLA_QK_PAD = 256
MLA_HQ = MLA_HEADS * MLA_QK_PAD
MLA_HV = MLA_HEADS * MLA_V
LOG2_E = 1.4426950408889634
MLA_TK = 512
MLA_TQ = 2048
MLA_QG = 256

LANES = 128
VMEM_LIMIT = 56 * 1024 * 1024
ROW_TILE = 512
SUB_ROWS = 256


def _cparams(n_grid):
    return pltpu.CompilerParams(
        dimension_semantics=("arbitrary",) * n_grid,
        vmem_limit_bytes=VMEM_LIMIT)


def _silu(x):
    return x / (1.0 + jnp.exp(-x))


def _rms(x, g):
    ms = jnp.mean(x * x, axis=-1, keepdims=True)
    return x * lax.rsqrt(ms + EPS) * g


def _norm_mod(x, g, shift, scale):
    return _rms(x, g) * (1.0 + scale) + shift


def _dot(a, b):
    return jnp.dot(a, b, preferred_element_type=F32)


def _dot_nt(a, b):
    return lax.dot_general(a, b, (((1,), (1,)), ((), ())), preferred_element_type=F32)


def _dot_tn(a, b):
    return lax.dot_general(a, b, (((0,), (0,)), ((), ())), preferred_element_type=F32)


def _mod_kernel(c_ref, w_ref, b_ref, o_ref):
    c = _silu(c_ref[...]).astype(BF16)
    o_ref[...] = _dot(c, w_ref[...].astype(BF16)) + b_ref[...]


def _ada_proj(c, w, b, tn=1024):
    n_layers, d, n = w.shape
    bt = c.shape[0]
    return pl.pallas_call(
        _mod_kernel,
        grid=(n_layers, n // tn),
        in_specs=[
            pl.BlockSpec((bt, d), lambda l, j: (0, 0)),
            pl.BlockSpec((None, d, tn), lambda l, j: (l, 0, j)),
            pl.BlockSpec((None, 1, tn), lambda l, j: (l, 0, j)),
        ],
        out_specs=pl.BlockSpec((None, bt, tn), lambda l, j: (l, 0, j)),
        out_shape=jax.ShapeDtypeStruct((n_layers, bt, n), F32),
        compiler_params=_cparams(2),
        name="ada_proj",
    )(c, w, b.reshape(n_layers, 1, n))


def _gla_out_residual(x, of_ref, ob_ref, r_ref, gm_ref, gn_ref, w_ref):
    o = of_ref[...] + ob_ref[...]
    gn = gn_ref[...]
    normed = [_rms(o[:, h * GLA_DV:(h + 1) * GLA_DV], gn) for h in range(GLA_HEADS)]
    z = (_silu(r_ref[...]) * jnp.concatenate(normed, axis=1)).astype(BF16)
    return x + gm_ref[...] * _dot(z, w_ref[...])


def _mla_out_residual(x, ao_ref, gm_ref, w_ref):
    return x + gm_ref[...] * _dot(ao_ref[...], w_ref[...])


def _gla_in_proj(x, sh_ref, sc_ref, g_ref, w_ref, wa_ref, q_ref, k_ref, v_ref, r_ref, a_ref):
    h = _norm_mod(x, g_ref[...], sh_ref[...], sc_ref[...]).astype(BF16)
    q_ref[...] = _dot(h, w_ref[:, 0:GLA_QK]) * (GLA_DK ** -0.5)
    k_ref[...] = _dot(h, w_ref[:, GLA_QK:2 * GLA_QK])
    v_ref[...] = _dot(h, w_ref[:, 2 * GLA_QK:2 * GLA_QK + GLA_VR]).astype(BF16)
    r_ref[...] = _dot(h, w_ref[:, 2 * GLA_QK + GLA_VR:])
    a_ref[...] = _dot(h, wa_ref[...]).astype(BF16)


def _rope(x, cos, sin_lo, sin_hi):
    half = MLA_ROPE // 2
    return (x * cos + pltpu.roll(x, LANES - half, 1) * sin_lo
            + pltpu.roll(x, half, 1) * sin_hi)


def _mla_in_proj(x, sh_ref, sc_ref, g_ref, wq_ref, wkv_ref, wkr_ref, gq_ref, gkv_ref,
                 wuq_ref, wuk_ref, wuvt_ref, cos_ref, slo_ref, shi_ref, q_ref, k_ref, vt_ref):
    h = _norm_mod(x, g_ref[...], sh_ref[...], sc_ref[...]).astype(BF16)
    c_q = _rms(_dot(h, wq_ref[...]), gq_ref[...]).astype(BF16)
    c_kv = _rms(_dot(h, wkv_ref[...]), gkv_ref[...]).astype(BF16)
    cos, slo, shi = cos_ref[...], slo_ref[...], shi_ref[...]
    k_rope = _rope(_dot(h, wkr_ref[...]), cos, slo, shi).astype(BF16)
    scale = (MLA_NOPE + MLA_ROPE) ** -0.5 * LOG2_E
    vt_ref[...] = _dot_nt(wuvt_ref[...], c_kv).astype(BF16)
    for hd in range(MLA_HEADS):
        lo = hd * MLA_QK_PAD
        q_h = _dot(c_q, wuq_ref[:, lo:lo + MLA_QK_PAD])
        q_ref[:, lo:lo + LANES] = (q_h[:, :LANES] * scale).astype(BF16)
        q_ref[:, lo + LANES:lo + MLA_QK_PAD] = (
            _rope(q_h[:, LANES:], cos, slo, shi) * scale).astype(BF16)
        k_ref[:, lo:lo + LANES] = _dot(
            c_kv, wuk_ref[:, hd * MLA_NOPE:(hd + 1) * MLA_NOPE]).astype(BF16)
        k_ref[:, lo + LANES:lo + MLA_QK_PAD] = k_rope


_N_PRE = {None: 0, "gla": 6, "mla": 3}
_N_POST_IN = {None: 0, "final": 3, "gla": 5, "mla": 14}
_N_FFN = 7


def _part_stages(refs, pre, post, row0, nrows):
    n0 = 1 + _N_PRE[pre]
    n1 = n0 + _N_FFN
    n2 = n1 + _N_POST_IN[post]
    rows = pl.ds(row0, nrows)
    rv = lambda r: r.at[rows]
    sh_ref, sc_ref, gt_ref, g_ref, wg_ref, wu_ref, wd_ref = refs[n0:n1]
    v = {}

    def load():
        x = refs[0][rows, :]
        if pre == "gla":
            of_ref, ob_ref, r_ref = (rv(r) for r in refs[1:4])
            x = _gla_out_residual(x, of_ref, ob_ref, r_ref, *refs[4:n0])
        elif pre == "mla":
            x = _mla_out_residual(x, rv(refs[1]), *refs[2:n0])
        v["x"] = x

    def norm():
        v["h"] = _norm_mod(v["x"], g_ref[...], sh_ref[...], sc_ref[...]).astype(BF16)

    def up():
        v["gate"] = _dot(v["h"], wg_ref[...])
        v["up"] = _dot(v["h"], wu_ref[...])

    def act():
        v["a"] = (_silu(v["gate"]) * v["up"]).astype(BF16)

    def down():
        y = v["x"] + (RES_HALF * gt_ref[...]) * _dot(v["a"], wd_ref[...])
        if post == "final":
            fsh_ref, fsc_ref, fg_ref = refs[n1:n2]
            y = _norm_mod(y, fg_ref[...], fsh_ref[...], fsc_ref[...])
        refs[n2][rows, :] = y
        v["y"] = y

    def proj():
        if post == "gla":
            _gla_in_proj(v["y"], *refs[n1:n2], *(rv(r) for r in refs[n2 + 1:]))
        elif post == "mla":
            q_ref, k_ref, vt_ref = refs[n2 + 1:]
            chunk, off = divmod(row0, MLA_TK)
            ins = list(refs[n1:n2 - 3]) + [rv(r) for r in refs[n2 - 3:n2]]
            _mla_in_proj(v["y"], *ins, rv(q_ref), rv(k_ref),
                         vt_ref.at[chunk, :, pl.ds(off, nrows)])

    return [load, norm, up, act, down, proj]


def _sublayer_kernel(*refs, pre, post):
    tm = refs[0].shape[0]
    nrows = min(SUB_ROWS, tm)
    parts = [_part_stages(refs, pre, post, r0, nrows) for r0 in range(0, tm, nrows)]
    n_st = len(parts[0])
    lag = 1
    order = []
    for t in range(n_st + lag * (len(parts) - 1)):
        for pi in reversed(range(len(parts))):
            if 0 <= t - lag * pi < n_st:
                order.append(parts[pi][t - lag * pi])
    for fn in order:
        fn()


def _row_spec(tm, d):
    return pl.BlockSpec((None, tm, d), lambda b, i: (b, i, 0))


def _vec_spec(d):
    return pl.BlockSpec((None, 1, d), lambda b, i: (b, 0, 0))


def _const_spec(shape):
    nd = len(shape)
    return pl.BlockSpec(shape, lambda b, i: (0,) * nd, pipeline_mode=pl.Buffered(1))


def _const_specs(arrays):
    return [_const_spec(a.shape) for a in arrays]


def _sublayer(x, ffn_args, pre=None, pre_args=(), post=None, post_args=()):
    bsz, s, d = x.shape
    tm = min(ROW_TILE, s)
    row = functools.partial(_row_spec, tm)
    vec3 = [_vec_spec(d)] * 3
    args, specs = [x], [row(d)]
    if pre == "gla":
        args += list(pre_args)
        specs += [row(GLA_VR)] * 3 + [_vec_spec(d)] + _const_specs(pre_args[4:])
    elif pre == "mla":
        args += list(pre_args)
        specs += [row(MLA_HV), _vec_spec(d)] + _const_specs(pre_args[2:])
    args += list(ffn_args)
    specs += vec3 + _const_specs(ffn_args[3:])
    args += list(post_args)
    x_out = jax.ShapeDtypeStruct(x.shape, F32)
    if post is None:
        out_shape, out_specs = x_out, row(d)
    elif post == "final":
        specs += [_vec_spec(d)] * 2 + _const_specs(post_args[2:])
        out_shape, out_specs = x_out, row(d)
    elif post == "gla":
        specs += [_vec_spec(d)] * 2 + _const_specs(post_args[2:])
        out_shape = [x_out,
                     jax.ShapeDtypeStruct((bsz, s, GLA_QK), F32),
                     jax.ShapeDtypeStruct((bsz, s, GLA_QK), F32),
                     jax.ShapeDtypeStruct((bsz, s, GLA_VR), BF16),
                     jax.ShapeDtypeStruct((bsz, s, GLA_VR), F32),
                     jax.ShapeDtypeStruct((bsz, s, LANES), BF16)]
        out_specs = [row(d), row(GLA_QK), row(GLA_QK), row(GLA_VR), row(GLA_VR), row(LANES)]
    else:
        assert tm % MLA_TK == 0
        tab = pl.BlockSpec((tm, LANES), lambda b, i: (i, 0))
        specs += [_vec_spec(d)] * 2 + _const_specs(post_args[2:11]) + [tab] * 3
        out_shape = [x_out,
                     jax.ShapeDtypeStruct((bsz, s, MLA_HQ), BF16),
                     jax.ShapeDtypeStruct((bsz, s, MLA_HQ), BF16),
                     jax.ShapeDtypeStruct((bsz, s // MLA_TK, MLA_HV, MLA_TK), BF16)]
        out_specs = [row(d), row(MLA_HQ), row(MLA_HQ),
                     pl.BlockSpec((None, tm // MLA_TK, MLA_HV, MLA_TK),
                                  lambda b, i: (b, i, 0, 0))]
    assert len(args) == 1 + _N_PRE[pre] + _N_FFN + _N_POST_IN[post]
    return pl.pallas_call(
        functools.partial(_sublayer_kernel, pre=pre, post=post),
        grid=(bsz, s // tm),
        in_specs=specs,
        out_specs=out_specs,
        out_shape=out_shape,
        compiler_params=_cparams(2),
        name="sublayer",
    )(*args)


def _log_sigmoid(x):
    return jnp.minimum(x, 0.0) - jnp.log(1.0 + jnp.exp(-jnp.abs(x)))


def _chunk_cumsum(tri2, la):
    hi = la.astype(BF16)
    lo = (la - hi.astype(F32)).astype(BF16)
    return _dot(tri2, jnp.concatenate([hi, lo], axis=0))


def _gla_batched(q_ref, k_ref, v_ref, la, tri, keep, last_row, order):
    c = GLA_CHUNK
    rows = [slice(i * c, (i + 1) * c) for i in order]
    b = [_chunk_cumsum(tri, la[r, :]) for r in rows]
    eb = [jnp.exp(x) for x in b]
    q_in = [(q_ref[r, :] * e).astype(BF16) for r, e in zip(rows, eb)]
    k_dec = [k_ref[r, :] * jnp.exp(-x) for r, x in zip(rows, b)]
    dec = [e[last_row:last_row + 1, :] for e in eb]
    k_out = [(kd * d).astype(BF16) for kd, d in zip(k_dec, dec)]
    a = [jnp.where(keep, _dot_nt(qi, kd.astype(BF16)), 0.0).astype(BF16)
         for qi, kd in zip(q_in, k_dec)]
    o_intra = [_dot(ai, v_ref[r, :]) for ai, r in zip(a, rows)]
    kv = [_dot_tn(v_ref[r, :], ko) for r, ko in zip(rows, k_out)]
    return list(zip(rows, q_in, o_intra, dec, kv))


def _gla_recur_step(chunk, state, o_ref):
    rows, q_in, o_intra, dec, kv = chunk
    o_ref[rows, :] = o_intra + _dot_nt(q_in, state.astype(BF16))
    return state * dec + kv


def _gla_core_kernel(qf_ref, kf_ref, vf_ref, af_ref, qb_ref, kb_ref, vb_ref, ab_ref,
                     wf_ref, wb_ref, bg_ref, of_ref, ob_ref, sf_ref, sb_ref, *, n_chunks):
    @pl.when(pl.program_id(2) == 0)
    def _():
        sf_ref[...] = jnp.zeros_like(sf_ref)
        sb_ref[...] = jnp.zeros_like(sb_ref)

    c = GLA_CHUNK
    row = lax.broadcasted_iota(jnp.int32, (c, c), 0)
    col = lax.broadcasted_iota(jnp.int32, (c, c), 1)
    lower = col <= row
    upper = col >= row
    row2 = lax.broadcasted_iota(jnp.int32, (c, 2 * c), 0)
    col2 = lax.broadcasted_iota(jnp.int32, (c, 2 * c), 1) % c
    tri_f = (col2 <= row2).astype(BF16)
    tri_b = (col2 >= row2).astype(BF16)
    la_f = _log_sigmoid(_dot(af_ref[...], wf_ref[...]) + bg_ref[0:1, :]) / GLA_TAU
    la_b = _log_sigmoid(_dot(ab_ref[...], wb_ref[...]) + bg_ref[1:2, :]) / GLA_TAU
    directions = (
        (qf_ref, kf_ref, vf_ref, la_f, tri_f, lower, c - 1, range(n_chunks), sf_ref, of_ref),
        (qb_ref, kb_ref, vb_ref, la_b, tri_b, upper, 0, range(n_chunks - 1, -1, -1),
         sb_ref, ob_ref))
    for *batched_args, state_ref, o_ref in directions:
        state = state_ref[...]
        for chunk in _gla_batched(*batched_args):
            state = _gla_recur_step(chunk, state, o_ref)
        state_ref[...] = state


def _gla_core(q, k, v, a, w_up_f, w_up_b, b_gate):
    bsz, s, _ = q.shape
    ts = min(GLA_BLOCK, s)
    nt = s // ts
    fwd = lambda b, h, t: (b, t, h)
    bwd = lambda b, h, t: (b, nt - 1 - t, h)
    fwd0 = lambda b, h, t: (b, t, 0)
    bwd0 = lambda b, h, t: (b, nt - 1 - t, 0)
    head = lambda b, h, t: (0, h)
    qk = lambda m: pl.BlockSpec((None, ts, GLA_DK), m)
    vv = lambda m: pl.BlockSpec((None, ts, GLA_DV), m)
    return pl.pallas_call(
        functools.partial(_gla_core_kernel, n_chunks=ts // GLA_CHUNK),
        grid=(bsz, GLA_HEADS, nt),
        in_specs=[
            qk(fwd), qk(fwd), vv(fwd), pl.BlockSpec((None, ts, LANES), fwd0),
            qk(bwd), qk(bwd), vv(bwd), pl.BlockSpec((None, ts, LANES), bwd0),
            pl.BlockSpec((LANES, GLA_DK), head), pl.BlockSpec((LANES, GLA_DK), head),
            pl.BlockSpec((2, GLA_DK), head),
        ],
        out_specs=[vv(fwd), vv(bwd)],
        out_shape=[jax.ShapeDtypeStruct((bsz, s, GLA_VR), F32)] * 2,
        scratch_shapes=[pltpu.VMEM((GLA_DV, GLA_DK), F32)] * 2,
        compiler_params=_cparams(3),
        name="gla_core",
    )(q, k, v, a, q, k, v, a, w_up_f, w_up_b, b_gate)


def _mla_attn_kernel(q_ref, k_ref, vt_ref, o_ref, sa_ref, sb_ref, *, tk, n_kv):
    tq = q_ref.shape[0]
    gw = MLA_QG
    n_g = tq // gw
    cols = [slice(g * gw, (g + 1) * gw) for g in range(n_g)]
    qs = [q_ref[c, :] for c in cols]

    def scores(j, dst_ref, g, m):
        start = pl.multiple_of(j * tk, tk)
        s = _dot_nt(k_ref[pl.ds(start, tk), :], qs[g])
        dst_ref[:, cols[g]] = s
        return jnp.maximum(m, jnp.max(s, axis=0, keepdims=True))

    def probs(src_ref, g, m_prev, m_cur, l):
        alpha = jnp.exp2(m_prev - m_cur)
        p = jnp.exp2(src_ref[:, cols[g]] - m_cur)
        return alpha, p.astype(BF16), l * alpha + jnp.sum(p, axis=0, keepdims=True)

    def pair(i, carry, last):
        j = 2 * i
        st = [dict(zip(("m_prev", "m_cur", "l", "acc"), c)) for c in carry]

        def run(g, op):
            d = st[g]
            if op == 0:
                d["m_1"] = scores(j + 1, sb_ref, g, d["m_cur"])
            elif op == 1:
                d["alpha"], d["p"], d["l"] = probs(sa_ref, g, d["m_prev"], d["m_cur"], d["l"])
            elif op == 2:
                d["acc"] = d["acc"] * d["alpha"] + _dot(vt_ref[j], d["p"])
            elif op == 3:
                d["m_2"] = d["m_1"] if last else scores(j + 2, sa_ref, g, d["m_1"])
            elif op == 4:
                d["alpha"], d["p"], d["l"] = probs(sb_ref, g, d["m_cur"], d["m_1"], d["l"])
            else:
                d["acc"] = d["acc"] * d["alpha"] + _dot(vt_ref[j + 1], d["p"])

        n_ops = 6
        for t in range(n_ops + n_g - 1):
            for g in range(n_g):
                if 0 <= t - g < n_ops:
                    run(g, t - g)
        return tuple((d["m_1"], d["m_2"], d["l"], d["acc"]) for d in st)

    neg = jnp.full((1, gw), -jnp.inf, F32)
    init = tuple((neg, scores(0, sa_ref, g, neg), jnp.zeros((1, gw), F32),
                  jnp.zeros((MLA_V, gw), F32)) for g in range(n_g))
    carry = lax.fori_loop(0, n_kv // 2 - 1, functools.partial(pair, last=False), init)
    final = pair(n_kv // 2 - 1, carry, True)
    for g in range(n_g):
        _, _, l, acc = final[g]
        o_ref[cols[g], :] = (acc / l).T.astype(BF16)


def _mla_attn(q, k, vt):
    bsz, s, _ = q.shape
    tq = min(MLA_TQ, s)
    n_kv, tk = vt.shape[1], vt.shape[3]
    assert n_kv % 2 == 0
    return pl.pallas_call(
        functools.partial(_mla_attn_kernel, tk=tk, n_kv=n_kv),
        grid=(bsz, MLA_HEADS, s // tq),
        in_specs=[
            pl.BlockSpec((None, tq, MLA_QK_PAD), lambda b, h, i: (b, i, h)),
            pl.BlockSpec((None, s, MLA_QK_PAD), lambda b, h, i: (b, 0, h)),
            pl.BlockSpec((None, n_kv, MLA_V, tk), lambda b, h, i: (b, 0, h, 0)),
        ],
        out_specs=pl.BlockSpec((None, tq, MLA_V), lambda b, h, i: (b, i, h)),
        out_shape=jax.ShapeDtypeStruct((bsz, s, MLA_HV), BF16),
        scratch_shapes=[pltpu.VMEM((tk, tq), F32)] * 2,
        compiler_params=_cparams(3),
        name="mla_attn",
    )(q, k, vt)


def _rope_tables(s):
    half = MLA_ROPE // 2
    inv = 1.0 / (ROPE_THETA ** (jnp.arange(0, MLA_ROPE, 2, dtype=F32) / MLA_ROPE))
    ang = jnp.arange(s, dtype=F32)[:, None] * inv[None, :]
    cos, sin = jnp.cos(ang), jnp.sin(ang)
    z = jnp.zeros((s, half), F32)
    z2 = jnp.zeros((s, LANES - MLA_ROPE), F32)
    return (jnp.concatenate([cos, cos, z2], axis=1),
            jnp.concatenate([-sin, z, z2], axis=1),
            jnp.concatenate([z, sin, z2], axis=1))


def _prep_gla(w_in, w_gate_up, b_gate, g_norm, w_out):
    n_main = 2 * GLA_QK + 2 * GLA_VR
    w_a = jnp.pad(w_in[:, n_main:], ((0, 0), (0, LANES - 2 * GLA_GATE_RANK)))
    r = GLA_GATE_RANK
    w_up_f = jnp.pad(w_gate_up[0], ((0, LANES - r), (0, 0)))
    w_up_b = jnp.pad(w_gate_up[1], ((r, LANES - 2 * r), (0, 0)))
    return dict(w_qkvr=w_in[:, :n_main].astype(BF16), w_a=w_a.astype(BF16),
                w_up_f=w_up_f.astype(BF16), w_up_b=w_up_b.astype(BF16),
                b_gate=b_gate, g_norm=g_norm.reshape(1, GLA_DV), w_out=w_out.astype(BF16))


def _prep_mla(w_in, g_q, g_kv, w_uq, w_ukv, w_out):
    w_kr = jnp.pad(w_in[:, MLA_Q_RANK + MLA_KV_RANK:], ((0, 0), (0, LANES - MLA_ROPE)))
    uq = w_uq.reshape(MLA_Q_RANK, MLA_HEADS, MLA_NOPE + MLA_ROPE)
    uq = jnp.pad(uq, ((0, 0), (0, 0), (0, MLA_QK_PAD - MLA_NOPE - MLA_ROPE)))
    ukv = w_ukv.reshape(MLA_KV_RANK, MLA_HEADS, MLA_NOPE + MLA_V)
    return dict(
        w_cq=w_in[:, :MLA_Q_RANK].astype(BF16),
        w_ckv=w_in[:, MLA_Q_RANK:MLA_Q_RANK + MLA_KV_RANK].astype(BF16),
        w_kr=w_kr.astype(BF16),
        g_q=g_q.reshape(1, MLA_Q_RANK), g_kv=g_kv.reshape(1, MLA_KV_RANK),
        w_uq=uq.reshape(MLA_Q_RANK, MLA_HQ).astype(BF16),
        w_uk=ukv[:, :, :MLA_NOPE].reshape(MLA_KV_RANK, MLA_HEADS * MLA_NOPE).astype(BF16),
        w_uvt=ukv[:, :, MLA_NOPE:].reshape(MLA_KV_RANK, MLA_HV).T.astype(BF16),
        w_out=w_out.astype(BF16))


def _trunk(x, mods, fin, norm_g, final_g, ffn_w, gla_p, mla_p):
    bsz, s, d = x.shape
    rope_tabs = _rope_tables(s)
    vec = lambda m: m.reshape(bsz, 1, d)
    wg, wu, wd = ffn_w
    for i in range(DEPTH):
        s1, sc1, g1, sm, scm, gm, s2, sc2, g2 = [
            vec(mods[i, :, j * d:(j + 1) * d]) for j in range(N_MOD)]
        ffn1 = (s1, sc1, g1, norm_g[i, 0][None], wg[i, 0], wu[i, 0], wd[i, 0])
        ffn2 = (s2, sc2, g2, norm_g[i, 2][None], wg[i, 1], wu[i, 1], wd[i, 1])
        j = i // 2
        if i % 2 == 0:
            p = gla_p[j]
            x, q, k, v, r, a = _sublayer(
                x, ffn1, post="gla",
                post_args=(sm, scm, norm_g[i, 1][None], p["w_qkvr"], p["w_a"]))
            o_f, o_b = _gla_core(q, k, v, a, p["w_up_f"], p["w_up_b"], p["b_gate"])
            pre, pre_args = "gla", (o_f, o_b, r, gm, p["g_norm"], p["w_out"])
        else:
            p = mla_p[j]
            x, q, k, vt = _sublayer(
                x, ffn1, post="mla",
                post_args=(sm, scm, norm_g[i, 1][None], p["w_cq"], p["w_ckv"], p["w_kr"],
                           p["g_q"], p["g_kv"], p["w_uq"], p["w_uk"], p["w_uvt"]) + rope_tabs)
            pre, pre_args = "mla", (_mla_attn(q, k, vt), gm, p["w_out"])
        if i == DEPTH - 1:
            x = _sublayer(x, ffn2, pre, pre_args, "final",
                          (vec(fin[:, :d]), vec(fin[:, d:]), final_g))
        else:
            x = _sublayer(x, ffn2, pre, pre_args)
    return x


def kernel(x_prompt, x_sample, c_prompt, c_sample, ada_w, ada_b, norm_g, ffn_w_gate, ffn_w_up, ffn_w_down, gla_w_in, gla_w_gate_up, gla_b_gate, gla_g_norm, gla_w_out, mla_w_in, mla_g_q, mla_g_kv, mla_w_uq, mla_w_ukv, mla_w_out, final_ada_w, final_ada_b, final_g):
    nb = x_prompt.shape[0]
    c_all = jnp.concatenate([c_prompt, c_sample], axis=0)
    mods = _ada_proj(c_all, ada_w, ada_b)
    fin = _ada_proj(c_all, final_ada_w[None], final_ada_b[None])[0]
    ffn_w = (ffn_w_gate.astype(BF16), ffn_w_up.astype(BF16), ffn_w_down.astype(BF16))
    gla_p = [_prep_gla(gla_w_in[j], gla_w_gate_up[j], gla_b_gate[j], gla_g_norm[j],
                       gla_w_out[j]) for j in range(gla_w_in.shape[0])]
    mla_p = [_prep_mla(mla_w_in[j], mla_g_q[j], mla_g_kv[j], mla_w_uq[j], mla_w_ukv[j],
                       mla_w_out[j]) for j in range(mla_w_in.shape[0])]
    fg = final_g.reshape(1, -1)
    y_prompt = _trunk(x_prompt, mods[:, :nb], fin[:nb], norm_g, fg, ffn_w, gla_p, mla_p)
    y_sample = _trunk(x_sample, mods[:, nb:], fin[nb:], norm_g, fg, ffn_w, gla_p, mla_p)
    return (y_prompt, y_sample)
```

```python
import functools

import jax
import jax.numpy as jnp
from jax import lax
from jax.experimental import pallas as pl
from jax.experimental.pallas import tpu as pltpu

F32 = jnp.float32
BF16 = jnp.bfloat16

D_MODEL = 1024
DEPTH = 4
D_FF = 2816
N_MOD = 9
EPS = 1e-6
RES_HALF = 0.5

GLA_HEADS = 4
GLA_DK = 128
GLA_DV = 256
GLA_GATE_RANK = 16
GLA_TAU = 16.0
GLA_CHUNK = 64
GLA_QK = GLA_HEADS * GLA_DK
GLA_VR = GLA_HEADS * GLA_DV
GLA_BLOCK = 1024

MLA_HEADS = 8
MLA_NOPE = 128
MLA_ROPE = 64
MLA_V = 128
MLA_Q_RANK = 256
MLA_KV_RANK = 128
ROPE_THETA = 10000.0
MLA_QK_PAD = 256
MLA_HQ = MLA_HEADS * MLA_QK_PAD
MLA_HV = MLA_HEADS * MLA_V
LOG2_E = 1.4426950408889634
MLA_TK = 512
MLA_TQ = 4096
MLA_QG = 256

LANES = 128
VMEM_LIMIT = 56 * 1024 * 1024
ROW_TILE = 512
SUB_ROWS = 256


def _cparams(n_grid):
    return pltpu.CompilerParams(
        dimension_semantics=("arbitrary",) * n_grid,
        vmem_limit_bytes=VMEM_LIMIT)


def _silu(x):
    return x / (1.0 + jnp.exp(-x))


def _rms(x, g):
    ms = jnp.mean(x * x, axis=-1, keepdims=True)
    return x * lax.rsqrt(ms + EPS) * g


def _norm_mod(x, g, shift, scale):
    return _rms(x, g) * (1.0 + scale) + shift


def _dot(a, b):
    return jnp.dot(a, b, preferred_element_type=F32)


def _dot_nt(a, b):
    return lax.dot_general(a, b, (((1,), (1,)), ((), ())), preferred_element_type=F32)


def _dot_tn(a, b):
    return lax.dot_general(a, b, (((0,), (0,)), ((), ())), preferred_element_type=F32)


def _mod_kernel(c_ref, w_ref, b_ref, o_ref):
    c = _silu(c_ref[...]).astype(BF16)
    o_ref[...] = _dot(c, w_ref[...].astype(BF16)) + b_ref[...]


def _ada_proj(c, w, b, tn=1024):
    n_layers, d, n = w.shape
    bt = c.shape[0]
    return pl.pallas_call(
        _mod_kernel,
        grid=(n_layers, n // tn),
        in_specs=[
            pl.BlockSpec((bt, d), lambda l, j: (0, 0)),
            pl.BlockSpec((None, d, tn), lambda l, j: (l, 0, j)),
            pl.BlockSpec((None, 1, tn), lambda l, j: (l, 0, j)),
        ],
        out_specs=pl.BlockSpec((None, bt, tn), lambda l, j: (l, 0, j)),
        out_shape=jax.ShapeDtypeStruct((n_layers, bt, n), F32),
        compiler_params=_cparams(2),
        name="ada_proj",
    )(c, w, b.reshape(n_layers, 1, n))


def _gla_out_residual(x, of_ref, ob_ref, r_ref, gm_ref, gn_ref, w_ref):
    o = of_ref[...] + ob_ref[...]
    gn = gn_ref[...]
    normed = [_rms(o[:, h * GLA_DV:(h + 1) * GLA_DV], gn) for h in range(GLA_HEADS)]
    z = (_silu(r_ref[...]) * jnp.concatenate(normed, axis=1)).astype(BF16)
    return x + gm_ref[...] * _dot(z, w_ref[...])


def _mla_out_residual(x, ao_ref, gm_ref, w_ref):
    return x + gm_ref[...] * _dot(ao_ref[...], w_ref[...])


def _gla_in_proj(x, sh_ref, sc_ref, g_ref, w_ref, wa_ref, q_ref, k_ref, v_ref, r_ref, a_ref):
    h = _norm_mod(x, g_ref[...], sh_ref[...], sc_ref[...]).astype(BF16)
    q_ref[...] = _dot(h, w_ref[:, 0:GLA_QK]) * (GLA_DK ** -0.5)
    k_ref[...] = _dot(h, w_ref[:, GLA_QK:2 * GLA_QK])
    v_ref[...] = _dot(h, w_ref[:, 2 * GLA_QK:2 * GLA_QK + GLA_VR]).astype(BF16)
    r_ref[...] = _dot(h, w_ref[:, 2 * GLA_QK + GLA_VR:])
    a_ref[...] = _dot(h, wa_ref[...]).astype(BF16)


def _rope(x, cos, sin_lo, sin_hi, axis):
    half = MLA_ROPE // 2
    return (x * cos + pltpu.roll(x, LANES - half, axis) * sin_lo
            + pltpu.roll(x, half, axis) * sin_hi)


def _mla_in_proj(x, sh_ref, sc_ref, g_ref, wq_ref, wkv_ref, wkr_ref, gq_ref, gkv_ref,
                 wuqt_ref, wuk_ref, wuvt_ref, cos_ref, slo_ref, shi_ref,
                 cost_ref, slot_ref, shit_ref, qt_ref, k_ref, vt_ref):
    h = _norm_mod(x, g_ref[...], sh_ref[...], sc_ref[...]).astype(BF16)
    c_q = _rms(_dot(h, wq_ref[...]), gq_ref[...]).astype(BF16)
    c_kv = _rms(_dot(h, wkv_ref[...]), gkv_ref[...]).astype(BF16)
    k_rope = _rope(_dot(h, wkr_ref[...]), cos_ref[...], slo_ref[...], shi_ref[...], 1).astype(BF16)
    scale = (MLA_NOPE + MLA_ROPE) ** -0.5 * LOG2_E
    vt_ref[...] = _dot_nt(wuvt_ref[...], c_kv).astype(BF16)
    cos_t, slo_t, shi_t = cost_ref[...], slot_ref[...], shit_ref[...]
    for hd in range(MLA_HEADS):
        lo = hd * MLA_QK_PAD
        qt_h = _dot_nt(wuqt_ref[lo:lo + MLA_QK_PAD, :], c_q)
        qt_ref[lo:lo + LANES, :] = (qt_h[:LANES, :] * scale).astype(BF16)
        qt_ref[lo + LANES:lo + MLA_QK_PAD, :] = (
            _rope(qt_h[LANES:, :], cos_t, slo_t, shi_t, 0) * scale).astype(BF16)
        k_ref[:, lo:lo + LANES] = _dot(
            c_kv, wuk_ref[:, hd * MLA_NOPE:(hd + 1) * MLA_NOPE]).astype(BF16)
        k_ref[:, lo + LANES:lo + MLA_QK_PAD] = k_rope


_N_PRE = {None: 0, "gla": 6, "mla": 3}
_N_POST_IN = {None: 0, "final": 3, "gla": 5, "mla": 17}
_N_FFN = 7


def _part_stages(refs, pre, post, row0, nrows):
    n0 = 1 + _N_PRE[pre]
    n1 = n0 + _N_FFN
    n2 = n1 + _N_POST_IN[post]
    rows = pl.ds(row0, nrows)
    rv = lambda r: r.at[rows]
    sh_ref, sc_ref, gt_ref, g_ref, wg_ref, wu_ref, wd_ref = refs[n0:n1]
    v = {}

    def load():
        x = refs[0][rows, :]
        if pre == "gla":
            of_ref, ob_ref, r_ref = (rv(r) for r in refs[1:4])
            x = _gla_out_residual(x, of_ref, ob_ref, r_ref, *refs[4:n0])
        elif pre == "mla":
            x = _mla_out_residual(x, rv(refs[1]), *refs[2:n0])
        v["x"] = x

    def norm():
        v["h"] = _norm_mod(v["x"], g_ref[...], sh_ref[...], sc_ref[...]).astype(BF16)

    def up():
        v["gate"] = _dot(v["h"], wg_ref[...])
        v["up"] = _dot(v["h"], wu_ref[...])

    def act():
        v["a"] = (_silu(v["gate"]) * v["up"]).astype(BF16)

    def down():
        y = v["x"] + (RES_HALF * gt_ref[...]) * _dot(v["a"], wd_ref[...])
        if post == "final":
            fsh_ref, fsc_ref, fg_ref = refs[n1:n2]
            y = _norm_mod(y, fg_ref[...], fsh_ref[...], fsc_ref[...])
        refs[n2][rows, :] = y
        v["y"] = y

    def proj():
        if post == "gla":
            _gla_in_proj(v["y"], *refs[n1:n2], *(rv(r) for r in refs[n2 + 1:]))
        elif post == "mla":
            qt_ref, k_ref, vt_ref = refs[n2 + 1:]
            chunk, off = divmod(row0, MLA_TK)
            ins = (list(refs[n1:n2 - 6]) + [rv(r) for r in refs[n2 - 6:n2 - 3]]
                   + [r.at[:, rows] for r in refs[n2 - 3:n2]])
            _mla_in_proj(v["y"], *ins, qt_ref.at[:, rows], rv(k_ref),
                         vt_ref.at[chunk, :, pl.ds(off, nrows)])

    return [load, norm, up, act, down, proj]


def _sublayer_kernel(*refs, pre, post):
    tm = refs[0].shape[0]
    nrows = min(SUB_ROWS, tm)
    parts = [_part_stages(refs, pre, post, r0, nrows) for r0 in range(0, tm, nrows)]
    n_st = len(parts[0])
    lag = 1
    order = []
    for t in range(n_st + lag * (len(parts) - 1)):
        for pi in reversed(range(len(parts))):
            if 0 <= t - lag * pi < n_st:
                order.append(parts[pi][t - lag * pi])
    for fn in order:
        fn()


def _row_spec(tm, d):
    return pl.BlockSpec((None, tm, d), lambda b, i: (b, i, 0))


def _vec_spec(d):
    return pl.BlockSpec((None, 1, d), lambda b, i: (b, 0, 0))


def _const_spec(shape):
    nd = len(shape)
    return pl.BlockSpec(shape, lambda b, i: (0,) * nd, pipeline_mode=pl.Buffered(1))


def _const_specs(arrays):
    return [_const_spec(a.shape) for a in arrays]


def _sublayer(x, ffn_args, pre=None, pre_args=(), post=None, post_args=()):
    bsz, s, d = x.shape
    tm = min(ROW_TILE, s)
    row = functools.partial(_row_spec, tm)
    vec3 = [_vec_spec(d)] * 3
    args, specs = [x], [row(d)]
    if pre == "gla":
        args += list(pre_args)
        specs += [row(GLA_VR)] * 3 + [_vec_spec(d)] + _const_specs(pre_args[4:])
    elif pre == "mla":
        args += list(pre_args)
        specs += [row(MLA_HV), _vec_spec(d)] + _const_specs(pre_args[2:])
    args += list(ffn_args)
    specs += vec3 + _const_specs(ffn_args[3:])
    args += list(post_args)
    x_out = jax.ShapeDtypeStruct(x.shape, F32)
    if post is None:
        out_shape, out_specs = x_out, row(d)
    elif post == "final":
        specs += [_vec_spec(d)] * 2 + _const_specs(post_args[2:])
        out_shape, out_specs = x_out, row(d)
    elif post == "gla":
        specs += [_vec_spec(d)] * 2 + _const_specs(post_args[2:])
        out_shape = [x_out,
                     jax.ShapeDtypeStruct((bsz, s, GLA_QK), F32),
                     jax.ShapeDtypeStruct((bsz, s, GLA_QK), F32),
                     jax.ShapeDtypeStruct((bsz, s, GLA_VR), BF16),
                     jax.ShapeDtypeStruct((bsz, s, GLA_VR), F32),
                     jax.ShapeDtypeStruct((bsz, s, LANES), BF16)]
        out_specs = [row(d), row(GLA_QK), row(GLA_QK), row(GLA_VR), row(GLA_VR), row(LANES)]
    else:
        assert tm % MLA_TK == 0
        tab = pl.BlockSpec((tm, LANES), lambda b, i: (i, 0))
        tab_t = pl.BlockSpec((LANES, tm), lambda b, i: (0, i))
        specs += ([_vec_spec(d)] * 2 + _const_specs(post_args[2:11]) + [tab] * 3
                  + [tab_t] * 3)
        out_shape = [x_out,
                     jax.ShapeDtypeStruct((bsz, MLA_HQ, s), BF16),
                     jax.ShapeDtypeStruct((bsz, s, MLA_HQ), BF16),
                     jax.ShapeDtypeStruct((bsz, s // MLA_TK, MLA_HV, MLA_TK), BF16)]
        out_specs = [row(d), pl.BlockSpec((None, MLA_HQ, tm), lambda b, i: (b, 0, i)),
                     row(MLA_HQ),
                     pl.BlockSpec((None, tm // MLA_TK, MLA_HV, MLA_TK),
                                  lambda b, i: (b, i, 0, 0))]
    assert len(args) == 1 + _N_PRE[pre] + _N_FFN + _N_POST_IN[post]
    return pl.pallas_call(
        functools.partial(_sublayer_kernel, pre=pre, post=post),
        grid=(bsz, s // tm),
        in_specs=specs,
        out_specs=out_specs,
        out_shape=out_shape,
        compiler_params=_cparams(2),
        name="sublayer",
    )(*args)


def _log_sigmoid(x):
    return jnp.minimum(x, 0.0) - jnp.log(1.0 + jnp.exp(-jnp.abs(x)))


def _chunk_cumsum(tri2, la):
    hi = la.astype(BF16)
    lo = (la - hi.astype(F32)).astype(BF16)
    return _dot(tri2, jnp.concatenate([hi, lo], axis=0))


def _gla_batched(q_ref, k_ref, v_ref, la, tri, keep, last_row, order):
    c = GLA_CHUNK
    rows = [slice(i * c, (i + 1) * c) for i in order]
    b = [_chunk_cumsum(tri, la[r, :]) for r in rows]
    eb = [jnp.exp(x) for x in b]
    q_in = [(q_ref[r, :] * e).astype(BF16) for r, e in zip(rows, eb)]
    k_dec = [k_ref[r, :] * jnp.exp(-x) for r, x in zip(rows, b)]
    dec = [e[last_row:last_row + 1, :] for e in eb]
    k_out = [(kd * d).astype(BF16) for kd, d in zip(k_dec, dec)]
    a = [jnp.where(keep, _dot_nt(qi, kd.astype(BF16)), 0.0).astype(BF16)
         for qi, kd in zip(q_in, k_dec)]
    o_intra = [_dot(ai, v_ref[r, :]) for ai, r in zip(a, rows)]
    kv = [_dot_tn(v_ref[r, :], ko) for r, ko in zip(rows, k_out)]
    return list(zip(rows, q_in, o_intra, dec, kv))


def _gla_recur_step(chunk, state, o_ref):
    rows, q_in, o_intra, dec, kv = chunk
    o_ref[rows, :] = o_intra + _dot_nt(q_in, state.astype(BF16))
    return state * dec + kv


def _gla_core_kernel(qf_ref, kf_ref, vf_ref, af_ref, qb_ref, kb_ref, vb_ref, ab_ref,
                     wf_ref, wb_ref, bg_ref, of_ref, ob_ref, sf_ref, sb_ref, *, n_chunks):
    @pl.when(pl.program_id(2) == 0)
    def _():
        sf_ref[...] = jnp.zeros_like(sf_ref)
        sb_ref[...] = jnp.zeros_like(sb_ref)

    c = GLA_CHUNK
    row = lax.broadcasted_iota(jnp.int32, (c, c), 0)
    col = lax.broadcasted_iota(jnp.int32, (c, c), 1)
    lower = col <= row
    upper = col >= row
    row2 = lax.broadcasted_iota(jnp.int32, (c, 2 * c), 0)
    col2 = lax.broadcasted_iota(jnp.int32, (c, 2 * c), 1) % c
    tri_f = (col2 <= row2).astype(BF16)
    tri_b = (col2 >= row2).astype(BF16)
    la_f = _log_sigmoid(_dot(af_ref[...], wf_ref[...]) + bg_ref[0:1, :]) / GLA_TAU
    la_b = _log_sigmoid(_dot(ab_ref[...], wb_ref[...]) + bg_ref[1:2, :]) / GLA_TAU
    directions = (
        (qf_ref, kf_ref, vf_ref, la_f, tri_f, lower, c - 1, range(n_chunks), sf_ref, of_ref),
        (qb_ref, kb_ref, vb_ref, la_b, tri_b, upper, 0, range(n_chunks - 1, -1, -1),
         sb_ref, ob_ref))
    for *batched_args, state_ref, o_ref in directions:
        state = state_ref[...]
        for chunk in _gla_batched(*batched_args):
            state = _gla_recur_step(chunk, state, o_ref)
        state_ref[...] = state


def _gla_core(q, k, v, a, w_up_f, w_up_b, b_gate):
    bsz, s, _ = q.shape
    ts = min(GLA_BLOCK, s)
    nt = s // ts
    fwd = lambda b, h, t: (b, t, h)
    bwd = lambda b, h, t: (b, nt - 1 - t, h)
    fwd0 = lambda b, h, t: (b, t, 0)
    bwd0 = lambda b, h, t: (b, nt - 1 - t, 0)
    head = lambda b, h, t: (0, h)
    qk = lambda m: pl.BlockSpec((None, ts, GLA_DK), m)
    vv = lambda m: pl.BlockSpec((None, ts, GLA_DV), m)
    return pl.pallas_call(
        functools.partial(_gla_core_kernel, n_chunks=ts // GLA_CHUNK),
        grid=(bsz, GLA_HEADS, nt),
        in_specs=[
            qk(fwd), qk(fwd), vv(fwd), pl.BlockSpec((None, ts, LANES), fwd0),
            qk(bwd), qk(bwd), vv(bwd), pl.BlockSpec((None, ts, LANES), bwd0),
            pl.BlockSpec((LANES, GLA_DK), head), pl.BlockSpec((LANES, GLA_DK), head),
            pl.BlockSpec((2, GLA_DK), head),
        ],
        out_specs=[vv(fwd), vv(bwd)],
        out_shape=[jax.ShapeDtypeStruct((bsz, s, GLA_VR), F32)] * 2,
        scratch_shapes=[pltpu.VMEM((GLA_DV, GLA_DK), F32)] * 2,
        compiler_params=_cparams(3),
        name="gla_core",
    )(q, k, v, a, q, k, v, a, w_up_f, w_up_b, b_gate)


def _mla_attn_kernel(qt_ref, k_ref, vt_ref, o_ref, sa_ref, sb_ref, *, tk, n_kv):
    tq = qt_ref.shape[1]
    gw = MLA_QG
    n_g = tq // gw
    cols = [slice(g * gw, (g + 1) * gw) for g in range(n_g)]
    qs = [qt_ref[:, c] for c in cols]

    def scores(j, dst_ref, g, m):
        start = pl.multiple_of(j * tk, tk)
        s = _dot(k_ref[pl.ds(start, tk), :], qs[g])
        dst_ref[:, cols[g]] = s
        return jnp.maximum(m, jnp.max(s, axis=0, keepdims=True))

    def probs(src_ref, g, m_prev, m_cur, l):
        alpha = jnp.exp2(m_prev - m_cur)
        p = jnp.exp2(src_ref[:, cols[g]] - m_cur)
        return alpha, p.astype(BF16), l * alpha + jnp.sum(p, axis=0, keepdims=True)

    def pair(i, carry, last):
        j = 2 * i
        st = [dict(zip(("m_prev", "m_cur", "l", "acc"), c)) for c in carry]

        def run(g, op):
            d = st[g]
            if op == 0:
                d["m_1"] = scores(j + 1, sb_ref, g, d["m_cur"])
            elif op == 1:
                d["alpha"], d["p"], d["l"] = probs(sa_ref, g, d["m_prev"], d["m_cur"], d["l"])
            elif op == 2:
                d["acc"] = d["acc"] * d["alpha"] + _dot(vt_ref[j], d["p"])
            elif op == 3:
                d["m_2"] = d["m_1"] if last else scores(j + 2, sa_ref, g, d["m_1"])
            elif op == 4:
                d["alpha"], d["p"], d["l"] = probs(sb_ref, g, d["m_cur"], d["m_1"], d["l"])
            else:
                d["acc"] = d["acc"] * d["alpha"] + _dot(vt_ref[j + 1], d["p"])

        n_ops = 6
        for t in range(n_ops + n_g - 1):
            for g in range(n_g):
                if 0 <= t - g < n_ops:
                    run(g, t - g)
        return tuple((d["m_1"], d["m_2"], d["l"], d["acc"]) for d in st)

    neg = jnp.full((1, gw), -jnp.inf, F32)
    init = tuple((neg, scores(0, sa_ref, g, neg), jnp.zeros((1, gw), F32),
                  jnp.zeros((MLA_V, gw), F32)) for g in range(n_g))
    carry = lax.fori_loop(0, n_kv // 2 - 1, functools.partial(pair, last=False), init)
    final = pair(n_kv // 2 - 1, carry, True)
    for g in range(n_g):
        _, _, l, acc = final[g]
        o_ref[cols[g], :] = (acc / l).T.astype(BF16)


def _mla_attn(qt, k, vt):
    bsz, s, _ = k.shape
    tq = min(MLA_TQ, s)
    n_kv, tk = vt.shape[1], vt.shape[3]
    assert n_kv % 2 == 0
    return pl.pallas_call(
        functools.partial(_mla_attn_kernel, tk=tk, n_kv=n_kv),
        grid=(bsz, MLA_HEADS, s // tq),
        in_specs=[
            pl.BlockSpec((None, MLA_QK_PAD, tq), lambda b, h, i: (b, h, i)),
            pl.BlockSpec((None, s, MLA_QK_PAD), lambda b, h, i: (b, 0, h)),
            pl.BlockSpec((None, n_kv, MLA_V, tk), lambda b, h, i: (b, 0, h, 0)),
        ],
        out_specs=pl.BlockSpec((None, tq, MLA_V), lambda b, h, i: (b, i, h)),
        out_shape=jax.ShapeDtypeStruct((bsz, s, MLA_HV), BF16),
        scratch_shapes=[pltpu.VMEM((tk, tq), F32)] * 2,
        compiler_params=_cparams(3),
        name="mla_attn",
    )(qt, k, vt)


def _rope_tables(s):
    half = MLA_ROPE // 2
    inv = 1.0 / (ROPE_THETA ** (jnp.arange(0, MLA_ROPE, 2, dtype=F32) / MLA_ROPE))
    ang = jnp.arange(s, dtype=F32)[:, None] * inv[None, :]
    cos, sin = jnp.cos(ang), jnp.sin(ang)
    z = jnp.zeros((s, half), F32)
    z2 = jnp.zeros((s, LANES - MLA_ROPE), F32)
    tabs = (jnp.concatenate([cos, cos, z2], axis=1),
            jnp.concatenate([-sin, z, z2], axis=1),
            jnp.concatenate([z, sin, z2], axis=1))
    return tabs + tuple(t.T for t in tabs)


def _prep_gla(w_in, w_gate_up, b_gate, g_norm, w_out):
    n_main = 2 * GLA_QK + 2 * GLA_VR
    w_a = jnp.pad(w_in[:, n_main:], ((0, 0), (0, LANES - 2 * GLA_GATE_RANK)))
    r = GLA_GATE_RANK
    w_up_f = jnp.pad(w_gate_up[0], ((0, LANES - r), (0, 0)))
    w_up_b = jnp.pad(w_gate_up[1], ((r, LANES - 2 * r), (0, 0)))
    return dict(w_qkvr=w_in[:, :n_main].astype(BF16), w_a=w_a.astype(BF16),
                w_up_f=w_up_f.astype(BF16), w_up_b=w_up_b.astype(BF16),
                b_gate=b_gate, g_norm=g_norm.reshape(1, GLA_DV), w_out=w_out.astype(BF16))


def _prep_mla(w_in, g_q, g_kv, w_uq, w_ukv, w_out):
    w_kr = jnp.pad(w_in[:, MLA_Q_RANK + MLA_KV_RANK:], ((0, 0), (0, LANES - MLA_ROPE)))
    uq = w_uq.reshape(MLA_Q_RANK, MLA_HEADS, MLA_NOPE + MLA_ROPE)
    uq = jnp.pad(uq, ((0, 0), (0, 0), (0, MLA_QK_PAD - MLA_NOPE - MLA_ROPE)))
    ukv = w_ukv.reshape(MLA_KV_RANK, MLA_HEADS, MLA_NOPE + MLA_V)
    return dict(
        w_cq=w_in[:, :MLA_Q_RANK].astype(BF16),
        w_ckv=w_in[:, MLA_Q_RANK:MLA_Q_RANK + MLA_KV_RANK].astype(BF16),
        w_kr=w_kr.astype(BF16),
        g_q=g_q.reshape(1, MLA_Q_RANK), g_kv=g_kv.reshape(1, MLA_KV_RANK),
        w_uqt=uq.reshape(MLA_Q_RANK, MLA_HQ).T.astype(BF16),
        w_uk=ukv[:, :, :MLA_NOPE].reshape(MLA_KV_RANK, MLA_HEADS * MLA_NOPE).astype(BF16),
        w_uvt=ukv[:, :, MLA_NOPE:].reshape(MLA_KV_RANK, MLA_HV).T.astype(BF16),
        w_out=w_out.astype(BF16))


def _trunk(x, mods, fin, norm_g, final_g, ffn_w, gla_p, mla_p):
    bsz, s, d = x.shape
    rope_tabs = _rope_tables(s)
    vec = lambda m: m.reshape(bsz, 1, d)
    wg, wu, wd = ffn_w
    for i in range(DEPTH):
        s1, sc1, g1, sm, scm, gm, s2, sc2, g2 = [
            vec(mods[i, :, j * d:(j + 1) * d]) for j in range(N_MOD)]
        ffn1 = (s1, sc1, g1, norm_g[i, 0][None], wg[i, 0], wu[i, 0], wd[i, 0])
        ffn2 = (s2, sc2, g2, norm_g[i, 2][None], wg[i, 1], wu[i, 1], wd[i, 1])
        j = i // 2
        if i % 2 == 0:
            p = gla_p[j]
            x, q, k, v, r, a = _sublayer(
                x, ffn1, post="gla",
                post_args=(sm, scm, norm_g[i, 1][None], p["w_qkvr"], p["w_a"]))
            o_f, o_b = _gla_core(q, k, v, a, p["w_up_f"], p["w_up_b"], p["b_gate"])
            pre, pre_args = "gla", (o_f, o_b, r, gm, p["g_norm"], p["w_out"])
        else:
            p = mla_p[j]
            x, qt, k, vt = _sublayer(
                x, ffn1, post="mla",
                post_args=(sm, scm, norm_g[i, 1][None], p["w_cq"], p["w_ckv"], p["w_kr"],
                           p["g_q"], p["g_kv"], p["w_uqt"], p["w_uk"], p["w_uvt"]) + rope_tabs)
            pre, pre_args = "mla", (_mla_attn(qt, k, vt), gm, p["w_out"])
        if i == DEPTH - 1:
            x = _sublayer(x, ffn2, pre, pre_args, "final",
                          (vec(fin[:, :d]), vec(fin[:, d:]), final_g))
        else:
            x = _sublayer(x, ffn2, pre, pre_args)
    return x


def kernel(x_prompt, x_sample, c_prompt, c_sample, ada_w, ada_b, norm_g, ffn_w_gate, ffn_w_up, ffn_w_down, gla_w_in, gla_w_gate_up, gla_b_gate, gla_g_norm, gla_w_out, mla_w_in, mla_g_q, mla_g_kv, mla_w_uq, mla_w_ukv, mla_w_out, final_ada_w, final_ada_b, final_g):
    nb = x_prompt.shape[0]
    c_all = jnp.concatenate([c_prompt, c_sample], axis=0)
    mods = _ada_proj(c_all, ada_w, ada_b)
    fin = _ada_proj(c_all, final_ada_w[None], final_ada_b[None])[0]
    ffn_w = (ffn_w_gate.astype(BF16), ffn_w_up.astype(BF16), ffn_w_down.astype(BF16))
    gla_p = [_prep_gla(gla_w_in[j], gla_w_gate_up[j], gla_b_gate[j], gla_g_norm[j],
                       gla_w_out[j]) for j in range(gla_w_in.shape[0])]
    mla_p = [_prep_mla(mla_w_in[j], mla_g_q[j], mla_g_kv[j], mla_w_uq[j], mla_w_ukv[j],
                       mla_w_out[j]) for j in range(mla_w_in.shape[0])]
    fg = final_g.reshape(1, -1)
    y_prompt = _trunk(x_prompt, mods[:, :nb], fin[:nb], norm_g, fg, ffn_w, gla_p, mla_p)
    y_sample = _trunk(x_sample, mods[:, nb:], fin[nb:], norm_g, fg, ffn_w, gla_p, mla_p)
    return (y_prompt, y_sample)
```

```python
import functools

import jax
import jax.numpy as jnp
from jax import lax
from jax.experimental import pallas as pl
from jax.experimental.pallas import tpu as pltpu

F32 = jnp.float32
BF16 = jnp.bfloat16

D_MODEL = 1024
DEPTH = 4
D_FF = 2816
N_MOD = 9
EPS = 1e-6
RES_HALF = 0.5

GLA_HEADS = 4
GLA_DK = 128
GLA_DV = 256
GLA_GATE_RANK = 16
GLA_TAU = 16.0
GLA_CHUNK = 64
GLA_QK = GLA_HEADS * GLA_DK
GLA_VR = GLA_HEADS * GLA_DV
GLA_BLOCK = 1024

MLA_HEADS = 8
MLA_NOPE = 128
MLA_ROPE = 64
MLA_V = 128
MLA_Q_RANK = 256
MLA_KV_RANK = 128
ROPE_THETA = 10000.0
MLA_QK_PAD = 256
MLA_HQ = MLA_HEADS * MLA_QK_PAD
MLA_HV = MLA_HEADS * MLA_V
LOG2_E = 1.4426950408889634
MLA_TK = 512
MLA_TQ = 4096
MLA_QG = 256

LANES = 128
VMEM_LIMIT = 56 * 1024 * 1024
ROW_TILE = 512
SUB_ROWS = 256


def _cparams(n_grid):
    return pltpu.CompilerParams(
        dimension_semantics=("arbitrary",) * n_grid,
        vmem_limit_bytes=VMEM_LIMIT)


def _silu(x):
    return x / (1.0 + jnp.exp(-x))


def _rms(x, g):
    ms = jnp.mean(x * x, axis=-1, keepdims=True)
    return x * lax.rsqrt(ms + EPS) * g


def _norm_mod(x, g, shift, scale):
    return _rms(x, g) * (1.0 + scale) + shift


def _dot(a, b):
    return jnp.dot(a, b, preferred_element_type=F32)


def _dot_nt(a, b):
    return lax.dot_general(a, b, (((1,), (1,)), ((), ())), preferred_element_type=F32)


def _dot_tn(a, b):
    return lax.dot_general(a, b, (((0,), (0,)), ((), ())), preferred_element_type=F32)


def _mod_kernel(c_ref, w_ref, b_ref, o_ref):
    c = _silu(c_ref[...]).astype(BF16)
    o_ref[...] = _dot(c, w_ref[...].astype(BF16)) + b_ref[...]


def _ada_proj(c, w, b, tn=1024):
    n_layers, d, n = w.shape
    bt = c.shape[0]
    return pl.pallas_call(
        _mod_kernel,
        grid=(n_layers, n // tn),
        in_specs=[
            pl.BlockSpec((bt, d), lambda l, j: (0, 0)),
            pl.BlockSpec((None, d, tn), lambda l, j: (l, 0, j)),
            pl.BlockSpec((None, 1, tn), lambda l, j: (l, 0, j)),
        ],
        out_specs=pl.BlockSpec((None, bt, tn), lambda l, j: (l, 0, j)),
        out_shape=jax.ShapeDtypeStruct((n_layers, bt, n), F32),
        compiler_params=_cparams(2),
        name="ada_proj",
    )(c, w, b.reshape(n_layers, 1, n))


def _gla_out_residual(x, of_ref, ob_ref, r_ref, gm_ref, gn_ref, w_ref):
    o = of_ref[...] + ob_ref[...]
    gn = gn_ref[...]
    normed = [_rms(o[:, h * GLA_DV:(h + 1) * GLA_DV], gn) for h in range(GLA_HEADS)]
    z = (_silu(r_ref[...]) * jnp.concatenate(normed, axis=1)).astype(BF16)
    return x + gm_ref[...] * _dot(z, w_ref[...])


def _mla_out_residual(x, ao_ref, gm_ref, w_ref):
    return x + gm_ref[...] * _dot(ao_ref[...], w_ref[...])


def _gla_in_proj(x, sh_ref, sc_ref, g_ref, w_ref, wa_ref, q_ref, k_ref, v_ref, r_ref, a_ref):
    h = _norm_mod(x, g_ref[...], sh_ref[...], sc_ref[...]).astype(BF16)
    q_ref[...] = _dot(h, w_ref[:, 0:GLA_QK]) * (GLA_DK ** -0.5)
    k_ref[...] = _dot(h, w_ref[:, GLA_QK:2 * GLA_QK])
    v_ref[...] = _dot(h, w_ref[:, 2 * GLA_QK:2 * GLA_QK + GLA_VR]).astype(BF16)
    r_ref[...] = _dot(h, w_ref[:, 2 * GLA_QK + GLA_VR:])
    a_ref[...] = _dot(h, wa_ref[...]).astype(BF16)


def _rope(x, cos, sin_lo, sin_hi, axis):
    half = MLA_ROPE // 2
    return (x * cos + pltpu.roll(x, LANES - half, axis) * sin_lo
            + pltpu.roll(x, half, axis) * sin_hi)


def _mla_in_proj(x, sh_ref, sc_ref, g_ref, wq_ref, wkv_ref, wkr_ref, gq_ref, gkv_ref,
                 wuqt_ref, wuk_ref, wuvt_ref, cos_ref, slo_ref, shi_ref,
                 cost_ref, slot_ref, shit_ref, qt_ref, k_ref, vt_ref):
    h = _norm_mod(x, g_ref[...], sh_ref[...], sc_ref[...]).astype(BF16)
    c_q = _rms(_dot(h, wq_ref[...]), gq_ref[...]).astype(BF16)
    c_kv = _rms(_dot(h, wkv_ref[...]), gkv_ref[...]).astype(BF16)
    k_rope = _rope(_dot(h, wkr_ref[...]), cos_ref[...], slo_ref[...], shi_ref[...], 1).astype(BF16)
    scale = (MLA_NOPE + MLA_ROPE) ** -0.5 * LOG2_E
    vt_ref[...] = _dot_nt(wuvt_ref[...], c_kv).astype(BF16)
    cos_t, slo_t, shi_t = cost_ref[...], slot_ref[...], shit_ref[...]
    qt = _dot_nt(wuqt_ref[...], c_q)
    k_nope = _dot(c_kv, wuk_ref[...]).astype(BF16)
    for hd in range(MLA_HEADS):
        lo = hd * MLA_QK_PAD
        qt_ref[lo:lo + LANES, :] = (qt[lo:lo + LANES, :] * scale).astype(BF16)
        qt_ref[lo + LANES:lo + MLA_QK_PAD, :] = (
            _rope(qt[lo + LANES:lo + MLA_QK_PAD, :], cos_t, slo_t, shi_t, 0) * scale
        ).astype(BF16)
        k_ref[:, lo:lo + LANES] = k_nope[:, hd * MLA_NOPE:(hd + 1) * MLA_NOPE]
        k_ref[:, lo + LANES:lo + MLA_QK_PAD] = k_rope


_N_PRE = {None: 0, "gla": 6, "mla": 3}
_N_POST_IN = {None: 0, "final": 3, "gla": 5, "mla": 17}
_N_FFN = 7


def _part_stages(refs, pre, post, row0, nrows):
    n0 = 1 + _N_PRE[pre]
    n1 = n0 + _N_FFN
    n2 = n1 + _N_POST_IN[post]
    rows = pl.ds(row0, nrows)
    rv = lambda r: r.at[rows]
    sh_ref, sc_ref, gt_ref, g_ref, wg_ref, wu_ref, wd_ref = refs[n0:n1]
    v = {}

    def load():
        x = refs[0][rows, :]
        if pre == "gla":
            of_ref, ob_ref, r_ref = (rv(r) for r in refs[1:4])
            x = _gla_out_residual(x, of_ref, ob_ref, r_ref, *refs[4:n0])
        elif pre == "mla":
            x = _mla_out_residual(x, rv(refs[1]), *refs[2:n0])
        v["x"] = x

    def norm():
        v["h"] = _norm_mod(v["x"], g_ref[...], sh_ref[...], sc_ref[...]).astype(BF16)

    def up():
        v["gate"] = _dot(v["h"], wg_ref[...])
        v["up"] = _dot(v["h"], wu_ref[...])

    def act():
        v["a"] = (_silu(v["gate"]) * v["up"]).astype(BF16)

    def down():
        y = v["x"] + (RES_HALF * gt_ref[...]) * _dot(v["a"], wd_ref[...])
        if post == "final":
            fsh_ref, fsc_ref, fg_ref = refs[n1:n2]
            y = _norm_mod(y, fg_ref[...], fsh_ref[...], fsc_ref[...])
        refs[n2][rows, :] = y
        v["y"] = y

    def proj():
        if post == "gla":
            _gla_in_proj(v["y"], *refs[n1:n2], *(rv(r) for r in refs[n2 + 1:]))
        elif post == "mla":
            qt_ref, k_ref, vt_ref = refs[n2 + 1:]
            chunk, off = divmod(row0, MLA_TK)
            ins = (list(refs[n1:n2 - 6]) + [rv(r) for r in refs[n2 - 6:n2 - 3]]
                   + [r.at[:, rows] for r in refs[n2 - 3:n2]])
            _mla_in_proj(v["y"], *ins, qt_ref.at[:, rows], rv(k_ref),
                         vt_ref.at[chunk, :, pl.ds(off, nrows)])

    return [load, norm, up, act, down, proj]


def _sublayer_kernel(*refs, pre, post):
    tm = refs[0].shape[0]
    nrows = min(SUB_ROWS, tm)
    parts = [_part_stages(refs, pre, post, r0, nrows) for r0 in range(0, tm, nrows)]
    n_st = len(parts[0])
    lag = 1
    order = []
    for t in range(n_st + lag * (len(parts) - 1)):
        for pi in reversed(range(len(parts))):
            if 0 <= t - lag * pi < n_st:
                order.append(parts[pi][t - lag * pi])
    for fn in order:
        fn()


def _row_spec(tm, d):
    return pl.BlockSpec((None, tm, d), lambda b, i: (b, i, 0))


def _vec_spec(d):
    return pl.BlockSpec((None, 1, d), lambda b, i: (b, 0, 0))


def _const_spec(shape):
    nd = len(shape)
    return pl.BlockSpec(shape, lambda b, i: (0,) * nd, pipeline_mode=pl.Buffered(1))


def _const_specs(arrays):
    return [_const_spec(a.shape) for a in arrays]


def _sublayer(x, ffn_args, pre=None, pre_args=(), post=None, post_args=()):
    bsz, s, d = x.shape
    tm = min(ROW_TILE, s)
    row = functools.partial(_row_spec, tm)
    vec3 = [_vec_spec(d)] * 3
    args, specs = [x], [row(d)]
    if pre == "gla":
        args += list(pre_args)
        specs += [row(GLA_VR)] * 3 + [_vec_spec(d)] + _const_specs(pre_args[4:])
    elif pre == "mla":
        args += list(pre_args)
        specs += [row(MLA_HV), _vec_spec(d)] + _const_specs(pre_args[2:])
    args += list(ffn_args)
    specs += vec3 + _const_specs(ffn_args[3:])
    args += list(post_args)
    x_out = jax.ShapeDtypeStruct(x.shape, F32)
    if post is None:
        out_shape, out_specs = x_out, row(d)
    elif post == "final":
        specs += [_vec_spec(d)] * 2 + _const_specs(post_args[2:])
        out_shape, out_specs = x_out, row(d)
    elif post == "gla":
        specs += [_vec_spec(d)] * 2 + _const_specs(post_args[2:])
        out_shape = [x_out,
                     jax.ShapeDtypeStruct((bsz, s, GLA_QK), F32),
                     jax.ShapeDtypeStruct((bsz, s, GLA_QK), F32),
                     jax.ShapeDtypeStruct((bsz, s, GLA_VR), BF16),
                     jax.ShapeDtypeStruct((bsz, s, GLA_VR), F32),
                     jax.ShapeDtypeStruct((bsz, s, LANES), BF16)]
        out_specs = [row(d), row(GLA_QK), row(GLA_QK), row(GLA_VR), row(GLA_VR), row(LANES)]
    else:
        assert tm % MLA_TK == 0
        tab = pl.BlockSpec((tm, LANES), lambda b, i: (i, 0))
        tab_t = pl.BlockSpec((LANES, tm), lambda b, i: (0, i))
        specs += ([_vec_spec(d)] * 2 + _const_specs(post_args[2:11]) + [tab] * 3
                  + [tab_t] * 3)
        out_shape = [x_out,
                     jax.ShapeDtypeStruct((bsz, MLA_HQ, s), BF16),
                     jax.ShapeDtypeStruct((bsz, s, MLA_HQ), BF16),
                     jax.ShapeDtypeStruct((bsz, s // MLA_TK, MLA_HV, MLA_TK), BF16)]
        out_specs = [row(d), pl.BlockSpec((None, MLA_HQ, tm), lambda b, i: (b, 0, i)),
                     row(MLA_HQ),
                     pl.BlockSpec((None, tm // MLA_TK, MLA_HV, MLA_TK),
                                  lambda b, i: (b, i, 0, 0))]
    assert len(args) == 1 + _N_PRE[pre] + _N_FFN + _N_POST_IN[post]
    return pl.pallas_call(
        functools.partial(_sublayer_kernel, pre=pre, post=post),
        grid=(bsz, s // tm),
        in_specs=specs,
        out_specs=out_specs,
        out_shape=out_shape,
        compiler_params=_cparams(2),
        name="sublayer",
    )(*args)


def _log_sigmoid(x):
    return jnp.minimum(x, 0.0) - jnp.log(1.0 + jnp.exp(-jnp.abs(x)))


def _chunk_cumsum(tri2, la):
    hi = la.astype(BF16)
    lo = (la - hi.astype(F32)).astype(BF16)
    return _dot(tri2, jnp.concatenate([hi, lo], axis=0))


def _gla_batched(q_ref, k_ref, v_ref, la, tri, keep, last_row, order):
    c = GLA_CHUNK
    rows = [slice(i * c, (i + 1) * c) for i in order]
    b = [_chunk_cumsum(tri, la[r, :]) for r in rows]
    eb = [jnp.exp(x) for x in b]
    q_in = [(q_ref[r, :] * e).astype(BF16) for r, e in zip(rows, eb)]
    k_dec = [k_ref[r, :] * jnp.exp(-x) for r, x in zip(rows, b)]
    dec = [e[last_row:last_row + 1, :] for e in eb]
    k_out = [(kd * d).astype(BF16) for kd, d in zip(k_dec, dec)]
    a = [jnp.where(keep, _dot_nt(qi, kd.astype(BF16)), 0.0).astype(BF16)
         for qi, kd in zip(q_in, k_dec)]
    o_intra = [_dot(ai, v_ref[r, :]) for ai, r in zip(a, rows)]
    kv = [_dot_tn(v_ref[r, :], ko) for r, ko in zip(rows, k_out)]
    return list(zip(rows, q_in, o_intra, dec, kv))


def _gla_recur_step(chunk, state, o_ref):
    rows, q_in, o_intra, dec, kv = chunk
    o_ref[rows, :] = o_intra + _dot(q_in, state.T.astype(BF16))
    return state * dec + kv


def _gla_core_kernel(qf_ref, kf_ref, vf_ref, af_ref, qb_ref, kb_ref, vb_ref, ab_ref,
                     wf_ref, wb_ref, bg_ref, of_ref, ob_ref, sf_ref, sb_ref, *, n_chunks):
    @pl.when(pl.program_id(2) == 0)
    def _():
        sf_ref[...] = jnp.zeros_like(sf_ref)
        sb_ref[...] = jnp.zeros_like(sb_ref)

    c = GLA_CHUNK
    row = lax.broadcasted_iota(jnp.int32, (c, c), 0)
    col = lax.broadcasted_iota(jnp.int32, (c, c), 1)
    lower = col <= row
    upper = col >= row
    row2 = lax.broadcasted_iota(jnp.int32, (c, 2 * c), 0)
    col2 = lax.broadcasted_iota(jnp.int32, (c, 2 * c), 1) % c
    tri_f = (col2 <= row2).astype(BF16)
    tri_b = (col2 >= row2).astype(BF16)
    la_f = _log_sigmoid(_dot(af_ref[...], wf_ref[...]) + bg_ref[0:1, :]) / GLA_TAU
    la_b = _log_sigmoid(_dot(ab_ref[...], wb_ref[...]) + bg_ref[1:2, :]) / GLA_TAU
    directions = (
        (qf_ref, kf_ref, vf_ref, la_f, tri_f, lower, c - 1, range(n_chunks), sf_ref, of_ref),
        (qb_ref, kb_ref, vb_ref, la_b, tri_b, upper, 0, range(n_chunks - 1, -1, -1),
         sb_ref, ob_ref))
    for *batched_args, state_ref, o_ref in directions:
        state = state_ref[...]
        for chunk in _gla_batched(*batched_args):
            state = _gla_recur_step(chunk, state, o_ref)
        state_ref[...] = state


def _gla_core(q, k, v, a, w_up_f, w_up_b, b_gate):
    bsz, s, _ = q.shape
    ts = min(GLA_BLOCK, s)
    nt = s // ts
    fwd = lambda b, h, t: (b, t, h)
    bwd = lambda b, h, t: (b, nt - 1 - t, h)
    fwd0 = lambda b, h, t: (b, t, 0)
    bwd0 = lambda b, h, t: (b, nt - 1 - t, 0)
    head = lambda b, h, t: (0, h)
    qk = lambda m: pl.BlockSpec((None, ts, GLA_DK), m)
    vv = lambda m: pl.BlockSpec((None, ts, GLA_DV), m)
    return pl.pallas_call(
        functools.partial(_gla_core_kernel, n_chunks=ts // GLA_CHUNK),
        grid=(bsz, GLA_HEADS, nt),
        in_specs=[
            qk(fwd), qk(fwd), vv(fwd), pl.BlockSpec((None, ts, LANES), fwd0),
            qk(bwd), qk(bwd), vv(bwd), pl.BlockSpec((None, ts, LANES), bwd0),
            pl.BlockSpec((LANES, GLA_DK), head), pl.BlockSpec((LANES, GLA_DK), head),
            pl.BlockSpec((2, GLA_DK), head),
        ],
        out_specs=[vv(fwd), vv(bwd)],
        out_shape=[jax.ShapeDtypeStruct((bsz, s, GLA_VR), F32)] * 2,
        scratch_shapes=[pltpu.VMEM((GLA_DV, GLA_DK), F32)] * 2,
        compiler_params=_cparams(3),
        name="gla_core",
    )(q, k, v, a, q, k, v, a, w_up_f, w_up_b, b_gate)


def _mla_attn_kernel(qt_ref, k_ref, vt_ref, o_ref, sa_ref, sb_ref, *, tk, n_kv):
    tq = qt_ref.shape[1]
    gw = MLA_QG
    n_g = tq // gw
    cols = [slice(g * gw, (g + 1) * gw) for g in range(n_g)]
    qs = [qt_ref[:, c] for c in cols]

    def scores(j, dst_ref, g, m):
        start = pl.multiple_of(j * tk, tk)
        s = _dot(k_ref[pl.ds(start, tk), :], qs[g])
        dst_ref[:, cols[g]] = s
        return jnp.maximum(m, jnp.max(s, axis=0, keepdims=True))

    def probs(src_ref, g, m_prev, m_cur, l):
        alpha = jnp.exp2(m_prev - m_cur)
        p = jnp.exp2(src_ref[:, cols[g]] - m_cur)
        return alpha, p.astype(BF16), l * alpha + jnp.sum(p, axis=0, keepdims=True)

    def pair(i, carry, last):
        j = 2 * i
        st = [dict(zip(("m_prev", "m_cur", "l", "acc"), c)) for c in carry]

        def run(g, op):
            d = st[g]
            if op == 0:
                d["m_1"] = scores(j + 1, sb_ref, g, d["m_cur"])
            elif op == 1:
                d["alpha"], d["p"], d["l"] = probs(sa_ref, g, d["m_prev"], d["m_cur"], d["l"])
            elif op == 2:
                d["acc"] = d["acc"] * d["alpha"] + _dot(vt_ref[j], d["p"])
            elif op == 3:
                d["m_2"] = d["m_1"] if last else scores(j + 2, sa_ref, g, d["m_1"])
            elif op == 4:
                d["alpha"], d["p"], d["l"] = probs(sb_ref, g, d["m_cur"], d["m_1"], d["l"])
            else:
                d["acc"] = d["acc"] * d["alpha"] + _dot(vt_ref[j + 1], d["p"])

        n_ops = 6
        for t in range(n_ops + n_g - 1):
            for g in range(n_g):
                if 0 <= t - g < n_ops:
                    run(g, t - g)
        return tuple((d["m_1"], d["m_2"], d["l"], d["acc"]) for d in st)

    neg = jnp.full((1, gw), -jnp.inf, F32)
    init = tuple((neg, scores(0, sa_ref, g, neg), jnp.zeros((1, gw), F32),
                  jnp.zeros((MLA_V, gw), F32)) for g in range(n_g))
    carry = lax.fori_loop(0, n_kv // 2 - 1, functools.partial(pair, last=False), init)
    final = pair(n_kv // 2 - 1, carry, True)
    for g in range(n_g):
        _, _, l, acc = final[g]
        o_ref[cols[g], :] = (acc / l).T.astype(BF16)


def _mla_attn(qt, k, vt):
    bsz, s, _ = k.shape
    tq = min(MLA_TQ, s)
    n_kv, tk = vt.shape[1], vt.shape[3]
    assert n_kv % 2 == 0
    return pl.pallas_call(
        functools.partial(_mla_attn_kernel, tk=tk, n_kv=n_kv),
        grid=(bsz, MLA_HEADS, s // tq),
        in_specs=[
            pl.BlockSpec((None, MLA_QK_PAD, tq), lambda b, h, i: (b, h, i)),
            pl.BlockSpec((None, s, MLA_QK_PAD), lambda b, h, i: (b, 0, h)),
            pl.BlockSpec((None, n_kv, MLA_V, tk), lambda b, h, i: (b, 0, h, 0)),
        ],
        out_specs=pl.BlockSpec((None, tq, MLA_V), lambda b, h, i: (b, i, h)),
        out_shape=jax.ShapeDtypeStruct((bsz, s, MLA_HV), BF16),
        scratch_shapes=[pltpu.VMEM((tk, tq), F32)] * 2,
        compiler_params=_cparams(3),
        name="mla_attn",
    )(qt, k, vt)


def _rope_tables(s):
    half = MLA_ROPE // 2
    inv = 1.0 / (ROPE_THETA ** (jnp.arange(0, MLA_ROPE, 2, dtype=F32) / MLA_ROPE))
    ang = jnp.arange(s, dtype=F32)[:, None] * inv[None, :]
    cos, sin = jnp.cos(ang), jnp.sin(ang)
    z = jnp.zeros((s, half), F32)
    z2 = jnp.zeros((s, LANES - MLA_ROPE), F32)
    tabs = (jnp.concatenate([cos, cos, z2], axis=1),
            jnp.concatenate([-sin, z, z2], axis=1),
            jnp.concatenate([z, sin, z2], axis=1))
    return tabs + tuple(t.T for t in tabs)


def _prep_gla(w_in, w_gate_up, b_gate, g_norm, w_out):
    n_main = 2 * GLA_QK + 2 * GLA_VR
    w_a = jnp.pad(w_in[:, n_main:], ((0, 0), (0, LANES - 2 * GLA_GATE_RANK)))
    r = GLA_GATE_RANK
    w_up_f = jnp.pad(w_gate_up[0], ((0, LANES - r), (0, 0)))
    w_up_b = jnp.pad(w_gate_up[1], ((r, LANES - 2 * r), (0, 0)))
    return dict(w_qkvr=w_in[:, :n_main].astype(BF16), w_a=w_a.astype(BF16),
                w_up_f=w_up_f.astype(BF16), w_up_b=w_up_b.astype(BF16),
                b_gate=b_gate, g_norm=g_norm.reshape(1, GLA_DV), w_out=w_out.astype(BF16))


def _prep_mla(w_in, g_q, g_kv, w_uq, w_ukv, w_out):
    w_kr = jnp.pad(w_in[:, MLA_Q_RANK + MLA_KV_RANK:], ((0, 0), (0, LANES - MLA_ROPE)))
    uq = w_uq.reshape(MLA_Q_RANK, MLA_HEADS, MLA_NOPE + MLA_ROPE)
    uq = jnp.pad(uq, ((0, 0), (0, 0), (0, MLA_QK_PAD - MLA_NOPE - MLA_ROPE)))
    ukv = w_ukv.reshape(MLA_KV_RANK, MLA_HEADS, MLA_NOPE + MLA_V)
    return dict(
        w_cq=w_in[:, :MLA_Q_RANK].astype(BF16),
        w_ckv=w_in[:, MLA_Q_RANK:MLA_Q_RANK + MLA_KV_RANK].astype(BF16),
        w_kr=w_kr.astype(BF16),
        g_q=g_q.reshape(1, MLA_Q_RANK), g_kv=g_kv.reshape(1, MLA_KV_RANK),
        w_uqt=uq.reshape(MLA_Q_RANK, MLA_HQ).T.astype(BF16),
        w_uk=ukv[:, :, :MLA_NOPE].reshape(MLA_KV_RANK, MLA_HEADS * MLA_NOPE).astype(BF16),
        w_uvt=ukv[:, :, MLA_NOPE:].reshape(MLA_KV_RANK, MLA_HV).T.astype(BF16),
        w_out=w_out.astype(BF16))


def _trunk(x, mods, fin, norm_g, final_g, ffn_w, gla_p, mla_p):
    bsz, s, d = x.shape
    rope_tabs = _rope_tables(s)
    vec = lambda m: m.reshape(bsz, 1, d)
    wg, wu, wd = ffn_w
    for i in range(DEPTH):
        s1, sc1, g1, sm, scm, gm, s2, sc2, g2 = [
            vec(mods[i, :, j * d:(j + 1) * d]) for j in range(N_MOD)]
        ffn1 = (s1, sc1, g1, norm_g[i, 0][None], wg[i, 0], wu[i, 0], wd[i, 0])
        ffn2 = (s2, sc2, g2, norm_g[i, 2][None], wg[i, 1], wu[i, 1], wd[i, 1])
        j = i // 2
        if i % 2 == 0:
            p = gla_p[j]
            x, q, k, v, r, a = _sublayer(
                x, ffn1, post="gla",
                post_args=(sm, scm, norm_g[i, 1][None], p["w_qkvr"], p["w_a"]))
            o_f, o_b = _gla_core(q, k, v, a, p["w_up_f"], p["w_up_b"], p["b_gate"])
            pre, pre_args = "gla", (o_f, o_b, r, gm, p["g_norm"], p["w_out"])
        else:
            p = mla_p[j]
            x, qt, k, vt = _sublayer(
                x, ffn1, post="mla",
                post_args=(sm, scm, norm_g[i, 1][None], p["w_cq"], p["w_ckv"], p["w_kr"],
                           p["g_q"], p["g_kv"], p["w_uqt"], p["w_uk"], p["w_uvt"]) + rope_tabs)
            pre, pre_args = "mla", (_mla_attn(qt, k, vt), gm, p["w_out"])
        if i == DEPTH - 1:
            x = _sublayer(x, ffn2, pre, pre_args, "final",
                          (vec(fin[:, :d]), vec(fin[:, d:]), final_g))
        else:
            x = _sublayer(x, ffn2, pre, pre_args)
    return x


def kernel(x_prompt, x_sample, c_prompt, c_sample, ada_w, ada_b, norm_g, ffn_w_gate, ffn_w_up, ffn_w_down, gla_w_in, gla_w_gate_up, gla_b_gate, gla_g_norm, gla_w_out, mla_w_in, mla_g_q, mla_g_kv, mla_w_uq, mla_w_ukv, mla_w_out, final_ada_w, final_ada_b, final_g):
    nb = x_prompt.shape[0]
    c_all = jnp.concatenate([c_prompt, c_sample], axis=0)
    mods = _ada_proj(c_all, ada_w, ada_b)
    fin = _ada_proj(c_all, final_ada_w[None], final_ada_b[None])[0]
    ffn_w = (ffn_w_gate.astype(BF16), ffn_w_up.astype(BF16), ffn_w_down.astype(BF16))
    gla_p = [_prep_gla(gla_w_in[j], gla_w_gate_up[j], gla_b_gate[j], gla_g_norm[j],
                       gla_w_out[j]) for j in range(gla_w_in.shape[0])]
    mla_p = [_prep_mla(mla_w_in[j], mla_g_q[j], mla_g_kv[j], mla_w_uq[j], mla_w_ukv[j],
                       mla_w_out[j]) for j in range(mla_w_in.shape[0])]
    fg = final_g.reshape(1, -1)
    y_prompt = _trunk(x_prompt, mods[:, :nb], fin[:nb], norm_g, fg, ffn_w, gla_p, mla_p)
    y_sample = _trunk(x_sample, mods[:, nb:], fin[nb:], norm_g, fg, ffn_w, gla_p, mla_p)
    return (y_prompt, y_sample)
```

```python
import functools

import jax
import jax.numpy as jnp
from jax import lax
from jax.experimental import pallas as pl
from jax.experimental.pallas import tpu as pltpu

F32 = jnp.float32
BF16 = jnp.bfloat16

D_MODEL = 1024
DEPTH = 4
D_FF = 2816
N_MOD = 9
EPS = 1e-6
RES_HALF = 0.5

GLA_HEADS = 4
GLA_DK = 128
GLA_DV = 256
GLA_GATE_RANK = 16
GLA_TAU = 16.0
GLA_CHUNK = 64
GLA_QK = GLA_HEADS * GLA_DK
GLA_VR = GLA_HEADS * GLA_DV
GLA_BLOCK = 1024

MLA_HEADS = 8
MLA_NOPE = 128
MLA_ROPE = 64
MLA_V = 128
MLA_Q_RANK = 256
MLA_KV_RANK = 128
ROPE_THETA = 10000.0
MLA_QK_PAD = 256
MLA_HQ = MLA_HEADS * MLA_QK_PAD
MLA_HV = MLA_HEADS * MLA_V
LOG2_E = 1.4426950408889634
MLA_TK = 512
MLA_TQ = 4096
MLA_QG = 256

LANES = 128
VMEM_LIMIT = 56 * 1024 * 1024
ROW_TILE = 512
SUB_ROWS = 256
FFN_CHUNK = 1024


def _cparams(n_grid):
    return pltpu.CompilerParams(
        dimension_semantics=("arbitrary",) * n_grid,
        vmem_limit_bytes=VMEM_LIMIT)


def _silu(x):
    return x / (1.0 + jnp.exp(-x))


def _rms(x, g):
    ms = jnp.mean(x * x, axis=-1, keepdims=True)
    return x * lax.rsqrt(ms + EPS) * g


def _norm_mod(x, g, shift, scale):
    return _rms(x, g) * (1.0 + scale) + shift


def _dot(a, b):
    return jnp.dot(a, b, preferred_element_type=F32)


def _dot_nt(a, b):
    return lax.dot_general(a, b, (((1,), (1,)), ((), ())), preferred_element_type=F32)


def _dot_tn(a, b):
    return lax.dot_general(a, b, (((0,), (0,)), ((), ())), preferred_element_type=F32)


def _mod_kernel(c_ref, w_ref, b_ref, o_ref):
    c = _silu(c_ref[...]).astype(BF16)
    o_ref[...] = _dot(c, w_ref[...].astype(BF16)) + b_ref[...]


def _ada_proj(c, w, b, tn=1024):
    n_layers, d, n = w.shape
    bt = c.shape[0]
    return pl.pallas_call(
        _mod_kernel,
        grid=(n_layers, n // tn),
        in_specs=[
            pl.BlockSpec((bt, d), lambda l, j: (0, 0)),
            pl.BlockSpec((None, d, tn), lambda l, j: (l, 0, j)),
            pl.BlockSpec((None, 1, tn), lambda l, j: (l, 0, j)),
        ],
        out_specs=pl.BlockSpec((None, bt, tn), lambda l, j: (l, 0, j)),
        out_shape=jax.ShapeDtypeStruct((n_layers, bt, n), F32),
        compiler_params=_cparams(2),
        name="ada_proj",
    )(c, w, b.reshape(n_layers, 1, n))


def _gla_out_residual(x, of_ref, ob_ref, r_ref, gm_ref, gn_ref, w_ref):
    o = of_ref[...] + ob_ref[...]
    gn = gn_ref[...]
    normed = [_rms(o[:, h * GLA_DV:(h + 1) * GLA_DV], gn) for h in range(GLA_HEADS)]
    z = (_silu(r_ref[...]) * jnp.concatenate(normed, axis=1)).astype(BF16)
    return x + gm_ref[...] * _dot(z, w_ref[...])


def _mla_out_residual(x, ao_ref, gm_ref, w_ref):
    return x + gm_ref[...] * _dot(ao_ref[...], w_ref[...])


def _gla_in_proj(x, sh_ref, sc_ref, g_ref, w_ref, wa_ref, q_ref, k_ref, v_ref, r_ref, a_ref):
    h = _norm_mod(x, g_ref[...], sh_ref[...], sc_ref[...]).astype(BF16)
    q_ref[...] = _dot(h, w_ref[:, 0:GLA_QK]) * (GLA_DK ** -0.5)
    k_ref[...] = _dot(h, w_ref[:, GLA_QK:2 * GLA_QK])
    v_ref[...] = _dot(h, w_ref[:, 2 * GLA_QK:2 * GLA_QK + GLA_VR]).astype(BF16)
    r_ref[...] = _dot(h, w_ref[:, 2 * GLA_QK + GLA_VR:])
    a_ref[...] = _dot(h, wa_ref[...]).astype(BF16)


def _rope(x, cos, sin_lo, sin_hi, axis):
    half = MLA_ROPE // 2
    return (x * cos + pltpu.roll(x, LANES - half, axis) * sin_lo
            + pltpu.roll(x, half, axis) * sin_hi)


def _mla_in_proj(x, sh_ref, sc_ref, g_ref, wq_ref, wkv_ref, wkr_ref, gq_ref, gkv_ref,
                 wuqt_ref, wuk_ref, wuvt_ref, cos_ref, slo_ref, shi_ref,
                 cost_ref, slot_ref, shit_ref, qt_ref, k_ref, vt_ref):
    h = _norm_mod(x, g_ref[...], sh_ref[...], sc_ref[...]).astype(BF16)
    c_q = _rms(_dot(h, wq_ref[...]), gq_ref[...]).astype(BF16)
    c_kv = _rms(_dot(h, wkv_ref[...]), gkv_ref[...]).astype(BF16)
    k_rope = _rope(_dot(h, wkr_ref[...]), cos_ref[...], slo_ref[...], shi_ref[...], 1).astype(BF16)
    scale = (MLA_NOPE + MLA_ROPE) ** -0.5 * LOG2_E
    vt_ref[...] = _dot_nt(wuvt_ref[...], c_kv).astype(BF16)
    cos_t, slo_t, shi_t = cost_ref[...], slot_ref[...], shit_ref[...]
    qt = _dot_nt(wuqt_ref[...], c_q)
    k_nope = _dot(c_kv, wuk_ref[...]).astype(BF16)
    for hd in range(MLA_HEADS):
        lo = hd * MLA_QK_PAD
        qt_ref[lo:lo + LANES, :] = (qt[lo:lo + LANES, :] * scale).astype(BF16)
        qt_ref[lo + LANES:lo + MLA_QK_PAD, :] = (
            _rope(qt[lo + LANES:lo + MLA_QK_PAD, :], cos_t, slo_t, shi_t, 0) * scale
        ).astype(BF16)
        k_ref[:, lo:lo + LANES] = k_nope[:, hd * MLA_NOPE:(hd + 1) * MLA_NOPE]
        k_ref[:, lo + LANES:lo + MLA_QK_PAD] = k_rope


_N_PRE = {None: 0, "gla": 6, "mla": 3}
_N_POST_IN = {None: 0, "final": 3, "gla": 5, "mla": 17}
_N_FFN = 7


def _part_stages(refs, pre, post, row0, nrows):
    n0 = 1 + _N_PRE[pre]
    n1 = n0 + _N_FFN
    n2 = n1 + _N_POST_IN[post]
    rows = pl.ds(row0, nrows)
    rv = lambda r: r.at[rows]
    sh_ref, sc_ref, gt_ref, g_ref, wg_ref, wu_ref, wd_ref = refs[n0:n1]
    v = {}

    def load():
        x = refs[0][rows, :]
        if pre == "gla":
            of_ref, ob_ref, r_ref = (rv(r) for r in refs[1:4])
            x = _gla_out_residual(x, of_ref, ob_ref, r_ref, *refs[4:n0])
        elif pre == "mla":
            x = _mla_out_residual(x, rv(refs[1]), *refs[2:n0])
        v["x"] = x

    def norm():
        v["h"] = _norm_mod(v["x"], g_ref[...], sh_ref[...], sc_ref[...]).astype(BF16)

    f = wg_ref.shape[1]
    f_chunks = [slice(lo, min(lo + FFN_CHUNK, f)) for lo in range(0, f, FFN_CHUNK)]

    def up_c(cs):
        def fn():
            v["gate"] = _dot(v["h"], wg_ref[:, cs])
            v["up"] = _dot(v["h"], wu_ref[:, cs])
        return fn

    def act_c():
        v["a"] = (_silu(v.pop("gate")) * v.pop("up")).astype(BF16)

    def down_c(cs):
        def fn():
            part = _dot(v.pop("a"), wd_ref[cs, :])
            v["ffn"] = part if "ffn" not in v else v["ffn"] + part
        return fn

    def up():
        up_c(f_chunks[0])()

    def act():
        for i, cs in enumerate(f_chunks):
            act_c()
            if i + 1 < len(f_chunks):
                up_c(f_chunks[i + 1])()
            down_c(cs)()

    def down():
        y = v["x"] + (RES_HALF * gt_ref[...]) * v.pop("ffn")
        if post == "final":
            fsh_ref, fsc_ref, fg_ref = refs[n1:n2]
            y = _norm_mod(y, fg_ref[...], fsh_ref[...], fsc_ref[...])
        refs[n2][rows, :] = y
        v["y"] = y

    def proj():
        if post == "gla":
            _gla_in_proj(v["y"], *refs[n1:n2], *(rv(r) for r in refs[n2 + 1:]))
        elif post == "mla":
            qt_ref, k_ref, vt_ref = refs[n2 + 1:]
            chunk, off = divmod(row0, MLA_TK)
            ins = (list(refs[n1:n2 - 6]) + [rv(r) for r in refs[n2 - 6:n2 - 3]]
                   + [r.at[:, rows] for r in refs[n2 - 3:n2]])
            _mla_in_proj(v["y"], *ins, qt_ref.at[:, rows], rv(k_ref),
                         vt_ref.at[chunk, :, pl.ds(off, nrows)])

    return [load, norm, up, act, down, proj]


def _sublayer_kernel(*refs, pre, post):
    tm = refs[0].shape[0]
    nrows = min(SUB_ROWS, tm)
    parts = [_part_stages(refs, pre, post, r0, nrows) for r0 in range(0, tm, nrows)]
    n_st = len(parts[0])
    lag = 1
    order = []
    for t in range(n_st + lag * (len(parts) - 1)):
        for pi in reversed(range(len(parts))):
            if 0 <= t - lag * pi < n_st:
                order.append(parts[pi][t - lag * pi])
    for fn in order:
        fn()


def _row_spec(tm, d):
    return pl.BlockSpec((None, tm, d), lambda b, i: (b, i, 0))


def _vec_spec(d):
    return pl.BlockSpec((None, 1, d), lambda b, i: (b, 0, 0))


def _const_spec(shape):
    nd = len(shape)
    return pl.BlockSpec(shape, lambda b, i: (0,) * nd, pipeline_mode=pl.Buffered(1))


def _const_specs(arrays):
    return [_const_spec(a.shape) for a in arrays]


def _sublayer(x, ffn_args, pre=None, pre_args=(), post=None, post_args=()):
    bsz, s, d = x.shape
    tm = min(ROW_TILE, s)
    row = functools.partial(_row_spec, tm)
    vec3 = [_vec_spec(d)] * 3
    args, specs = [x], [row(d)]
    if pre == "gla":
        args += list(pre_args)
        specs += [row(GLA_VR)] * 3 + [_vec_spec(d)] + _const_specs(pre_args[4:])
    elif pre == "mla":
        args += list(pre_args)
        specs += [row(MLA_HV), _vec_spec(d)] + _const_specs(pre_args[2:])
    args += list(ffn_args)
    specs += vec3 + _const_specs(ffn_args[3:])
    args += list(post_args)
    x_out = jax.ShapeDtypeStruct(x.shape, F32)
    if post is None:
        out_shape, out_specs = x_out, row(d)
    elif post == "final":
        specs += [_vec_spec(d)] * 2 + _const_specs(post_args[2:])
        out_shape, out_specs = x_out, row(d)
    elif post == "gla":
        specs += [_vec_spec(d)] * 2 + _const_specs(post_args[2:])
        out_shape = [x_out,
                     jax.ShapeDtypeStruct((bsz, s, GLA_QK), F32),
                     jax.ShapeDtypeStruct((bsz, s, GLA_QK), F32),
                     jax.ShapeDtypeStruct((bsz, s, GLA_VR), BF16),
                     jax.ShapeDtypeStruct((bsz, s, GLA_VR), F32),
                     jax.ShapeDtypeStruct((bsz, s, LANES), BF16)]
        out_specs = [row(d), row(GLA_QK), row(GLA_QK), row(GLA_VR), row(GLA_VR), row(LANES)]
    else:
        assert tm % MLA_TK == 0
        tab = pl.BlockSpec((tm, LANES), lambda b, i: (i, 0))
        tab_t = pl.BlockSpec((LANES, tm), lambda b, i: (0, i))
        specs += ([_vec_spec(d)] * 2 + _const_specs(post_args[2:11]) + [tab] * 3
                  + [tab_t] * 3)
        out_shape = [x_out,
                     jax.ShapeDtypeStruct((bsz, MLA_HQ, s), BF16),
                     jax.ShapeDtypeStruct((bsz, s, MLA_HQ), BF16),
                     jax.ShapeDtypeStruct((bsz, s // MLA_TK, MLA_HV, MLA_TK), BF16)]
        out_specs = [row(d), pl.BlockSpec((None, MLA_HQ, tm), lambda b, i: (b, 0, i)),
                     row(MLA_HQ),
                     pl.BlockSpec((None, tm // MLA_TK, MLA_HV, MLA_TK),
                                  lambda b, i: (b, i, 0, 0))]
    assert len(args) == 1 + _N_PRE[pre] + _N_FFN + _N_POST_IN[post]
    return pl.pallas_call(
        functools.partial(_sublayer_kernel, pre=pre, post=post),
        grid=(bsz, s // tm),
        in_specs=specs,
        out_specs=out_specs,
        out_shape=out_shape,
        compiler_params=_cparams(2),
        name="sublayer",
    )(*args)


def _log_sigmoid(x):
    return jnp.minimum(x, 0.0) - jnp.log(1.0 + jnp.exp(-jnp.abs(x)))


def _chunk_cumsum(tri2, la):
    hi = la.astype(BF16)
    lo = (la - hi.astype(F32)).astype(BF16)
    return _dot(tri2, jnp.concatenate([hi, lo], axis=0))


def _gla_batched(q_ref, k_ref, v_ref, la, tri, keep, last_row, order):
    c = GLA_CHUNK
    rows = [slice(i * c, (i + 1) * c) for i in order]
    b = [_chunk_cumsum(tri, la[r, :]) for r in rows]
    eb = [jnp.exp(x) for x in b]
    q_in = [(q_ref[r, :] * e).astype(BF16) for r, e in zip(rows, eb)]
    k_dec = [k_ref[r, :] * jnp.exp(-x) for r, x in zip(rows, b)]
    dec = [e[last_row:last_row + 1, :] for e in eb]
    k_out = [(kd * d).astype(BF16) for kd, d in zip(k_dec, dec)]
    a = [jnp.where(keep, _dot_nt(qi, kd.astype(BF16)), 0.0).astype(BF16)
         for qi, kd in zip(q_in, k_dec)]
    o_intra = [_dot(ai, v_ref[r, :]) for ai, r in zip(a, rows)]
    kv = [_dot_tn(v_ref[r, :], ko) for r, ko in zip(rows, k_out)]
    return list(zip(rows, q_in, o_intra, dec, kv))


def _gla_recur_step(chunk, state, o_ref):
    rows, q_in, o_intra, dec, kv = chunk
    o_ref[rows, :] = o_intra + _dot(q_in, state.T.astype(BF16))
    return state * dec + kv


def _gla_core_kernel(qf_ref, kf_ref, vf_ref, af_ref, qb_ref, kb_ref, vb_ref, ab_ref,
                     wf_ref, wb_ref, bg_ref, of_ref, ob_ref, sf_ref, sb_ref, *, n_chunks):
    @pl.when(pl.program_id(2) == 0)
    def _():
        sf_ref[...] = jnp.zeros_like(sf_ref)
        sb_ref[...] = jnp.zeros_like(sb_ref)

    c = GLA_CHUNK
    row = lax.broadcasted_iota(jnp.int32, (c, c), 0)
    col = lax.broadcasted_iota(jnp.int32, (c, c), 1)
    lower = col <= row
    upper = col >= row
    row2 = lax.broadcasted_iota(jnp.int32, (c, 2 * c), 0)
    col2 = lax.broadcasted_iota(jnp.int32, (c, 2 * c), 1) % c
    tri_f = (col2 <= row2).astype(BF16)
    tri_b = (col2 >= row2).astype(BF16)
    la_f = _log_sigmoid(_dot(af_ref[...], wf_ref[...]) + bg_ref[0:1, :]) / GLA_TAU
    la_b = _log_sigmoid(_dot(ab_ref[...], wb_ref[...]) + bg_ref[1:2, :]) / GLA_TAU
    directions = (
        (qf_ref, kf_ref, vf_ref, la_f, tri_f, lower, c - 1, range(n_chunks), sf_ref, of_ref),
        (qb_ref, kb_ref, vb_ref, la_b, tri_b, upper, 0, range(n_chunks - 1, -1, -1),
         sb_ref, ob_ref))
    for *batched_args, state_ref, o_ref in directions:
        state = state_ref[...]
        for chunk in _gla_batched(*batched_args):
            state = _gla_recur_step(chunk, state, o_ref)
        state_ref[...] = state


def _gla_core(q, k, v, a, w_up_f, w_up_b, b_gate):
    bsz, s, _ = q.shape
    ts = min(GLA_BLOCK, s)
    nt = s // ts
    fwd = lambda b, h, t: (b, t, h)
    bwd = lambda b, h, t: (b, nt - 1 - t, h)
    fwd0 = lambda b, h, t: (b, t, 0)
    bwd0 = lambda b, h, t: (b, nt - 1 - t, 0)
    head = lambda b, h, t: (0, h)
    qk = lambda m: pl.BlockSpec((None, ts, GLA_DK), m)
    vv = lambda m: pl.BlockSpec((None, ts, GLA_DV), m)
    return pl.pallas_call(
        functools.partial(_gla_core_kernel, n_chunks=ts // GLA_CHUNK),
        grid=(bsz, GLA_HEADS, nt),
        in_specs=[
            qk(fwd), qk(fwd), vv(fwd), pl.BlockSpec((None, ts, LANES), fwd0),
            qk(bwd), qk(bwd), vv(bwd), pl.BlockSpec((None, ts, LANES), bwd0),
            pl.BlockSpec((LANES, GLA_DK), head), pl.BlockSpec((LANES, GLA_DK), head),
            pl.BlockSpec((2, GLA_DK), head),
        ],
        out_specs=[vv(fwd), vv(bwd)],
        out_shape=[jax.ShapeDtypeStruct((bsz, s, GLA_VR), F32)] * 2,
        scratch_shapes=[pltpu.VMEM((GLA_DV, GLA_DK), F32)] * 2,
        compiler_params=_cparams(3),
        name="gla_core",
    )(q, k, v, a, q, k, v, a, w_up_f, w_up_b, b_gate)


def _mla_attn_kernel(qt_ref, k_ref, vt_ref, o_ref, sa_ref, sb_ref, *, tk, n_kv):
    tq = qt_ref.shape[1]
    gw = MLA_QG
    n_g = tq // gw
    cols = [slice(g * gw, (g + 1) * gw) for g in range(n_g)]
    qs = [qt_ref[:, c] for c in cols]

    def scores(j, dst_ref, g, m):
        start = pl.multiple_of(j * tk, tk)
        s = _dot(k_ref[pl.ds(start, tk), :], qs[g])
        dst_ref[:, cols[g]] = s
        return jnp.maximum(m, jnp.max(s, axis=0, keepdims=True))

    def probs(src_ref, g, m_prev, m_cur, l):
        alpha = jnp.exp2(m_prev - m_cur)
        p = jnp.exp2(src_ref[:, cols[g]] - m_cur)
        return alpha, p.astype(BF16), l * alpha + jnp.sum(p, axis=0, keepdims=True)

    def pair(i, carry, last):
        j = 2 * i
        st = [dict(zip(("m_prev", "m_cur", "l", "acc"), c)) for c in carry]

        def run(g, op):
            d = st[g]
            if op == 0:
                d["m_1"] = scores(j + 1, sb_ref, g, d["m_cur"])
            elif op == 1:
                d["alpha"], d["p"], d["l"] = probs(sa_ref, g, d["m_prev"], d["m_cur"], d["l"])
            elif op == 2:
                d["acc"] = d["acc"] * d["alpha"] + _dot(vt_ref[j], d["p"])
            elif op == 3:
                d["m_2"] = d["m_1"] if last else scores(j + 2, sa_ref, g, d["m_1"])
            elif op == 4:
                d["alpha"], d["p"], d["l"] = probs(sb_ref, g, d["m_cur"], d["m_1"], d["l"])
            else:
                d["acc"] = d["acc"] * d["alpha"] + _dot(vt_ref[j + 1], d["p"])

        n_ops = 6
        for t in range(n_ops + n_g - 1):
            for g in range(n_g):
                if 0 <= t - g < n_ops:
                    run(g, t - g)
        return tuple((d["m_1"], d["m_2"], d["l"], d["acc"]) for d in st)

    neg = jnp.full((1, gw), -jnp.inf, F32)
    init = tuple((neg, scores(0, sa_ref, g, neg), jnp.zeros((1, gw), F32),
                  jnp.zeros((MLA_V, gw), F32)) for g in range(n_g))
    carry = lax.fori_loop(0, n_kv // 2 - 1, functools.partial(pair, last=False), init)
    final = pair(n_kv // 2 - 1, carry, True)
    for g in range(n_g):
        _, _, l, acc = final[g]
        o_ref[cols[g], :] = (acc / l).T.astype(BF16)


def _mla_attn(qt, k, vt):
    bsz, s, _ = k.shape
    tq = min(MLA_TQ, s)
    n_kv, tk = vt.shape[1], vt.shape[3]
    assert n_kv % 2 == 0
    return pl.pallas_call(
        functools.partial(_mla_attn_kernel, tk=tk, n_kv=n_kv),
        grid=(bsz, MLA_HEADS, s // tq),
        in_specs=[
            pl.BlockSpec((None, MLA_QK_PAD, tq), lambda b, h, i: (b, h, i)),
            pl.BlockSpec((None, s, MLA_QK_PAD), lambda b, h, i: (b, 0, h)),
            pl.BlockSpec((None, n_kv, MLA_V, tk), lambda b, h, i: (b, 0, h, 0)),
        ],
        out_specs=pl.BlockSpec((None, tq, MLA_V), lambda b, h, i: (b, i, h)),
        out_shape=jax.ShapeDtypeStruct((bsz, s, MLA_HV), BF16),
        scratch_shapes=[pltpu.VMEM((tk, tq), F32)] * 2,
        compiler_params=_cparams(3),
        name="mla_attn",
    )(qt, k, vt)


def _rope_tables(s):
    half = MLA_ROPE // 2
    inv = 1.0 / (ROPE_THETA ** (jnp.arange(0, MLA_ROPE, 2, dtype=F32) / MLA_ROPE))
    ang = jnp.arange(s, dtype=F32)[:, None] * inv[None, :]
    cos, sin = jnp.cos(ang), jnp.sin(ang)
    z = jnp.zeros((s, half), F32)
    z2 = jnp.zeros((s, LANES - MLA_ROPE), F32)
    tabs = (jnp.concatenate([cos, cos, z2], axis=1),
            jnp.concatenate([-sin, z, z2], axis=1),
            jnp.concatenate([z, sin, z2], axis=1))
    return tabs + tuple(t.T for t in tabs)


def _prep_gla(w_in, w_gate_up, b_gate, g_norm, w_out):
    n_main = 2 * GLA_QK + 2 * GLA_VR
    w_a = jnp.pad(w_in[:, n_main:], ((0, 0), (0, LANES - 2 * GLA_GATE_RANK)))
    r = GLA_GATE_RANK
    w_up_f = jnp.pad(w_gate_up[0], ((0, LANES - r), (0, 0)))
    w_up_b = jnp.pad(w_gate_up[1], ((r, LANES - 2 * r), (0, 0)))
    return dict(w_qkvr=w_in[:, :n_main].astype(BF16), w_a=w_a.astype(BF16),
                w_up_f=w_up_f.astype(BF16), w_up_b=w_up_b.astype(BF16),
                b_gate=b_gate, g_norm=g_norm.reshape(1, GLA_DV), w_out=w_out.astype(BF16))


def _prep_mla(w_in, g_q, g_kv, w_uq, w_ukv, w_out):
    w_kr = jnp.pad(w_in[:, MLA_Q_RANK + MLA_KV_RANK:], ((0, 0), (0, LANES - MLA_ROPE)))
    uq = w_uq.reshape(MLA_Q_RANK, MLA_HEADS, MLA_NOPE + MLA_ROPE)
    uq = jnp.pad(uq, ((0, 0), (0, 0), (0, MLA_QK_PAD - MLA_NOPE - MLA_ROPE)))
    ukv = w_ukv.reshape(MLA_KV_RANK, MLA_HEADS, MLA_NOPE + MLA_V)
    return dict(
        w_cq=w_in[:, :MLA_Q_RANK].astype(BF16),
        w_ckv=w_in[:, MLA_Q_RANK:MLA_Q_RANK + MLA_KV_RANK].astype(BF16),
        w_kr=w_kr.astype(BF16),
        g_q=g_q.reshape(1, MLA_Q_RANK), g_kv=g_kv.reshape(1, MLA_KV_RANK),
        w_uqt=uq.reshape(MLA_Q_RANK, MLA_HQ).T.astype(BF16),
        w_uk=ukv[:, :, :MLA_NOPE].reshape(MLA_KV_RANK, MLA_HEADS * MLA_NOPE).astype(BF16),
        w_uvt=ukv[:, :, MLA_NOPE:].reshape(MLA_KV_RANK, MLA_HV).T.astype(BF16),
        w_out=w_out.astype(BF16))


def _trunk(x, mods, fin, norm_g, final_g, ffn_w, gla_p, mla_p):
    bsz, s, d = x.shape
    rope_tabs = _rope_tables(s)
    vec = lambda m: m.reshape(bsz, 1, d)
    wg, wu, wd = ffn_w
    for i in range(DEPTH):
        s1, sc1, g1, sm, scm, gm, s2, sc2, g2 = [
            vec(mods[i, :, j * d:(j + 1) * d]) for j in range(N_MOD)]
        ffn1 = (s1, sc1, g1, norm_g[i, 0][None], wg[i, 0], wu[i, 0], wd[i, 0])
        ffn2 = (s2, sc2, g2, norm_g[i, 2][None], wg[i, 1], wu[i, 1], wd[i, 1])
        j = i // 2
        if i % 2 == 0:
            p = gla_p[j]
            x, q, k, v, r, a = _sublayer(
                x, ffn1, post="gla",
                post_args=(sm, scm, norm_g[i, 1][None], p["w_qkvr"], p["w_a"]))
            o_f, o_b = _gla_core(q, k, v, a, p["w_up_f"], p["w_up_b"], p["b_gate"])
            pre, pre_args = "gla", (o_f, o_b, r, gm, p["g_norm"], p["w_out"])
        else:
            p = mla_p[j]
            x, qt, k, vt = _sublayer(
                x, ffn1, post="mla",
                post_args=(sm, scm, norm_g[i, 1][None], p["w_cq"], p["w_ckv"], p["w_kr"],
                           p["g_q"], p["g_kv"], p["w_uqt"], p["w_uk"], p["w_uvt"]) + rope_tabs)
            pre, pre_args = "mla", (_mla_attn(qt, k, vt), gm, p["w_out"])
        if i == DEPTH - 1:
            x = _sublayer(x, ffn2, pre, pre_args, "final",
                          (vec(fin[:, :d]), vec(fin[:, d:]), final_g))
        else:
            x = _sublayer(x, ffn2, pre, pre_args)
    return x


def kernel(x_prompt, x_sample, c_prompt, c_sample, ada_w, ada_b, norm_g, ffn_w_gate, ffn_w_up, ffn_w_down, gla_w_in, gla_w_gate_up, gla_b_gate, gla_g_norm, gla_w_out, mla_w_in, mla_g_q, mla_g_kv, mla_w_uq, mla_w_ukv, mla_w_out, final_ada_w, final_ada_b, final_g):
    nb = x_prompt.shape[0]
    c_all = jnp.concatenate([c_prompt, c_sample], axis=0)
    mods = _ada_proj(c_all, ada_w, ada_b)
    fin = _ada_proj(c_all, final_ada_w[None], final_ada_b[None])[0]
    ffn_w = (ffn_w_gate.astype(BF16), ffn_w_up.astype(BF16), ffn_w_down.astype(BF16))
    gla_p = [_prep_gla(gla_w_in[j], gla_w_gate_up[j], gla_b_gate[j], gla_g_norm[j],
                       gla_w_out[j]) for j in range(gla_w_in.shape[0])]
    mla_p = [_prep_mla(mla_w_in[j], mla_g_q[j], mla_g_kv[j], mla_w_uq[j], mla_w_ukv[j],
                       mla_w_out[j]) for j in range(mla_w_in.shape[0])]
    fg = final_g.reshape(1, -1)
    y_prompt = _trunk(x_prompt, mods[:, :nb], fin[:nb], norm_g, fg, ffn_w, gla_p, mla_p)
    y_sample = _trunk(x_sample, mods[:, nb:], fin[nb:], norm_g, fg, ffn_w, gla_p, mla_p)
    return (y_prompt, y_sample)
```
